```python
import jax, jax.numpy as jnp
from jax import lax
import numpy as np

D_MODEL = 1024
BATCH = 16
SEQ = 2048
DEPTH = 1

HGRN_HEADS = 8
HGRN_KEY_DIM = 128
HGRN_VAL_DIM = D_MODEL // HGRN_HEADS
HGRN_FWIDTH = HGRN_HEADS * HGRN_KEY_DIM
HGRN_WIDTH = HGRN_HEADS * HGRN_VAL_DIM
CHUNK = 16
POOL_WINDOWS = (2, 4, 8, 16)
POOL_GROUPS = len(POOL_WINDOWS)
POOL_WIDTH = D_MODEL
POOL_GROUP_DIM = POOL_WIDTH // POOL_GROUPS
D_FF = -(-(8 * D_MODEL) // (3 * 256)) * 256
RMS_EPS = 1e-6
IN_SPLITS = (HGRN_FWIDTH, HGRN_FWIDTH, HGRN_FWIDTH, HGRN_WIDTH, HGRN_WIDTH, POOL_WIDTH, D_MODEL, D_MODEL)
IN_WIDTH = sum(IN_SPLITS)

kernel_name = "hgrn2_multipool_gated_hybrid_encoder"


def rmsnorm(x, g):
    xf = x.astype(jnp.float32)
    y = xf * lax.rsqrt(jnp.mean(xf * xf, axis=-1, keepdims=True) + RMS_EPS)
    return (y * g.astype(jnp.float32)).astype(x.dtype)


def chunk_gated_recurrence(q, k, v, log_f):
    B, H, L, N = q.shape
    Dv = v.shape[-1]
    n_chunks = L // CHUNK

    def to_chunks(t):
        return jnp.moveaxis(t.reshape(B, H, n_chunks, CHUNK, t.shape[-1]), 2, 0)

    qc, kc, vc, gc = to_chunks(q), to_chunks(k), to_chunks(v), to_chunks(log_f)
    mask = jnp.tril(jnp.ones((CHUNK, CHUNK), dtype=bool))

    def step(S, inp):
        qi, ki, vi, gi = inp
        b = jnp.cumsum(gi, axis=-2)
        b_last = b[..., -1:, :]
        q_dec = qi * jnp.exp(b)
        k_inv = ki * jnp.exp(-b)
        scores = jnp.einsum('bhin,bhjn->bhij', q_dec, k_inv)
        scores = jnp.where(mask, scores, 0.0)
        o = (jnp.einsum('bhij,bhjd->bhid', scores, vi)
             + jnp.einsum('bhin,bhnd->bhid', q_dec, S))
        k_end = ki * jnp.exp(b_last - b)
        S = (jnp.exp(b_last)[..., 0, :, None] * S
             + jnp.einsum('bhjn,bhjd->bhnd', k_end, vi))
        return S, o

    S0 = jnp.zeros((B, H, N, Dv), q.dtype)
    _, o = lax.scan(step, S0, (qc, kc, vc, gc))
    return jnp.moveaxis(o, 0, 2).reshape(B, H, L, Dv)


def hgrn2_bidirectional(q_raw, ff_raw, fb_raw, i_raw, og_raw, lb, norm_g):
    B, L, _ = q_raw.shape
    f32 = jnp.float32

    def heads(t, d):
        return t.reshape(B, L, HGRN_HEADS, d).transpose(0, 2, 1, 3)

    q = heads(jax.nn.silu(q_raw.astype(f32)), HGRN_KEY_DIM)
    v = heads(i_raw.astype(f32), HGRN_VAL_DIM)
    f_fwd = lb[0] + (1.0 - lb[0]) * jax.nn.sigmoid(ff_raw.astype(f32))
    f_bwd = lb[1] + (1.0 - lb[1]) * jax.nn.sigmoid(fb_raw.astype(f32))
    f = jnp.concatenate([heads(f_fwd, HGRN_KEY_DIM), heads(f_bwd, HGRN_KEY_DIM)[:, :, ::-1]], axis=1)
    qq = jnp.concatenate([q, q[:, :, ::-1]], axis=1)
    vv = jnp.concatenate([v, v[:, :, ::-1]], axis=1)
    o = chunk_gated_recurrence(qq, 1.0 - f, vv, jnp.log(f))
    o = o[:, :HGRN_HEADS] + o[:, HGRN_HEADS:, ::-1]
    o = o * lax.rsqrt(jnp.mean(o * o, axis=-1, keepdims=True) + RMS_EPS)
    o = o.transpose(0, 2, 1, 3).reshape(B, L, HGRN_WIDTH) * norm_g.astype(f32)
    return (o * jax.nn.silu(og_raw.astype(f32))).astype(q_raw.dtype)


def multiscale_pool(p, w_grp, scale):
    B, L, _ = p.shape
    f32 = jnp.float32
    pg = p.astype(f32).reshape(B, L, POOL_GROUPS, POOL_GROUP_DIM)
    cs = jnp.concatenate([jnp.zeros((B, 1, POOL_GROUPS, POOL_GROUP_DIM), f32),
                          jnp.cumsum(pg, axis=1)], axis=1)
    half = jnp.array([w // 2 for w in POOL_WINDOWS], dtype=jnp.int32)
    t = jnp.arange(L, dtype=jnp.int32)[:, None]
    lo = jnp.clip(t - half + 1, 0, L)
    hi = jnp.clip(t + half + 1, 0, L)
    gidx = jnp.arange(POOL_GROUPS, dtype=jnp.int32)[None, :]
    win_sum = cs[:, hi, gidx, :] - cs[:, lo, gidx, :]
    count = (hi - lo).astype(f32)[..., None]
    y = win_sum / count - pg
    y = jnp.einsum('blgc,gcd->blgd', y, w_grp.astype(f32))
    return (y.reshape(B, L, POOL_WIDTH) * scale.astype(f32)).astype(p.dtype)


def setup_inputs(seed: int = 0) -> dict:
    key = jax.random.key(seed)
    ks = jax.random.split(key, 16)
    f32 = jnp.float32
    nrm = lambda k, shape, fan_in: jax.random.normal(k, shape, f32) * (fan_in ** -0.5)
    gain = lambda k, shape: 1.0 + 0.02 * jax.random.normal(k, shape, f32)
    return {
        "x": jax.random.normal(ks[0], (BATCH, SEQ, D_MODEL), f32),
        "g_mix": gain(ks[1], (DEPTH, D_MODEL)),
        "w_in": nrm(ks[2], (DEPTH, D_MODEL, IN_WIDTH), D_MODEL),
        "lb_logits": 0.1 * jax.random.normal(ks[3], (2, DEPTH + 1, HGRN_FWIDTH), f32),
        "hgrn_norm_g": gain(ks[4], (DEPTH, HGRN_WIDTH)),
        "pool_w": nrm(ks[5], (DEPTH, POOL_GROUPS, POOL_GROUP_DIM, POOL_GROUP_DIM), POOL_GROUP_DIM),
        "pool_scale": gain(ks[6], (DEPTH, POOL_WIDTH)),
        "w_branch_a": nrm(ks[7], (DEPTH, HGRN_WIDTH, D_MODEL), HGRN_WIDTH),
        "w_branch_b": nrm(ks[8], (DEPTH, POOL_WIDTH, D_MODEL), POOL_WIDTH),
        "w_out": nrm(ks[9], (DEPTH, D_MODEL, D_MODEL), D_MODEL),
        "g_ffn": gain(ks[10], (DEPTH, D_MODEL)),
        "w_ffn_in": nrm(ks[11], (DEPTH, D_MODEL, 2 * D_FF), D_MODEL),
        "w_ffn_out": nrm(ks[12], (DEPTH, D_FF, D_MODEL), D_FF),
        "g_final": gain(ks[13], (D_MODEL,)),
    }


def reference(x, g_mix, w_in, lb_logits, hgrn_norm_g, pool_w, pool_scale, w_branch_a,
              w_branch_b, w_out, g_ffn, w_ffn_in, w_ffn_out, g_final):
    lb_all = jnp.cumsum(jax.nn.softmax(lb_logits.astype(jnp.float32), axis=1), axis=1)
    offsets = np.cumsum(IN_SPLITS)[:-1].tolist()
    h = x
    for l in range(DEPTH):
        u = rmsnorm(h, g_mix[l])
        proj = jnp.einsum('bsd,de->bse', u, w_in[l])
        q_r, ff_r, fb_r, i_r, og_r, p_r, ga_r, gb_r = jnp.split(proj, offsets, axis=-1)
        y_a = hgrn2_bidirectional(q_r, ff_r, fb_r, i_r, og_r, lb_all[:, l], hgrn_norm_g[l])
        y_b = multiscale_pool(p_r, pool_w[l], pool_scale[l])
        z_a = jnp.einsum('bse,ed->bsd', y_a, w_branch_a[l])
        z_b = jnp.einsum('bse,ed->bsd', y_b, w_branch_b[l])
        merged = jax.nn.sigmoid(ga_r) * z_a + jax.nn.sigmoid(gb_r) * z_b
        h = h + jnp.einsum('bsd,de->bse', merged, w_out[l])
        u = rmsnorm(h, g_ffn[l])
        gate, up = jnp.split(jnp.einsum('bsd,df->bsf', u, w_ffn_in[l]), 2, axis=-1)
        h = h + jnp.einsum('bsf,fd->bsd', jax.nn.silu(gate) * up, w_ffn_out[l])
    return rmsnorm(h, g_final)
```

```python
import functools

import numpy as np
import jax
import jax.numpy as jnp
from jax import lax
from jax.experimental import pallas as pl
from jax.experimental.pallas import tpu as pltpu

F32 = jnp.float32
BF16 = jnp.bfloat16

RMS_EPS = 1e-6
HEADS = 8
HEAD_DIM = 128
POOL_WINDOWS = (2, 4, 8, 16)
CHUNK = 128
BLOCK = 32
N_EXP = 5
V7X_VMEM_LIMIT = 56 * 1024 * 1024

_NT = (((1,), (1,)), ((), ()))
_TN = (((0,), (0,)), ((), ()))


def _sigmoid(x):
    return 1.0 / (1.0 + jnp.exp(-x))


def _cumsum_matrices():
    C, Bk = CHUNK, BLOCK
    w = np.zeros((2, N_EXP * C, C), np.float32)
    for t in range(C):
        b0 = (t // Bk) * Bk
        blk = t // Bk
        mid = b0 + Bk // 2 - 1
        if t > mid:
            w[0, t, mid + 1:t + 1] = 1.0
        else:
            w[0, t, t + 1:mid + 1] = -1.0
        if blk % 2 == 1:
            w[0, C + t, b0:t + 1] = 1.0
        else:
            w[0, C + t, t + 1:b0 + Bk] = 1.0
        if t >= C // 2:
            w[0, 2 * C + t, C // 2:t + 1] = 1.0
        else:
            w[0, 2 * C + t, t + 1:C // 2] = 1.0
        w[0, 3 * C + t, :t + 1] = 1.0
        w[0, 4 * C + t, t + 1:] = 1.0
        mid = b0 + Bk // 2
        if t < mid:
            w[1, t, t:mid] = 1.0
        else:
            w[1, t, mid:t] = -1.0
        if blk % 2 == 0:
            w[1, C + t, t:b0 + Bk] = 1.0
        else:
            w[1, C + t, b0:t] = 1.0
        if t < C // 2:
            w[1, 2 * C + t, t:C // 2] = 1.0
        else:
            w[1, 2 * C + t, C // 2:t] = 1.0
        w[1, 3 * C + t, t:] = 1.0
        w[1, 4 * C + t, :t] = 1.0
    return w


def _split_hi_lo(a):
    hi = a.astype(BF16)
    lo = (a - hi.astype(F32)).astype(BF16)
    return jnp.concatenate([hi, lo], axis=1)


def _rmsnorm_kernel(x_ref, g_ref, o_ref):
    x = x_ref[...]
    ms = jnp.mean(x * x, axis=-1, keepdims=True)
    o_ref[...] = (x * lax.rsqrt(ms + RMS_EPS) * g_ref[...]).astype(o_ref.dtype)


def _rmsnorm(x2, g, tile):
    T, D = x2.shape
    return pl.pallas_call(
        _rmsnorm_kernel,
        grid=(T // tile,),
        in_specs=[pl.BlockSpec((tile, D), lambda i: (i, 0)),
                  pl.BlockSpec((1, D), lambda i: (0, 0))],
        out_specs=pl.BlockSpec((tile, D), lambda i: (i, 0)),
        out_shape=jax.ShapeDtypeStruct((T, D), BF16),
        compiler_params=pltpu.CompilerParams(dimension_semantics=("arbitrary",)),
        name="rmsnorm",
    )(x2, g.reshape(1, D))


def _hgrn_kernel(u_ref, w_ref, lbl_ref, ng_ref, wcum_ref, ya_ref, yp_ref,
                 qs, gf, gb, kf, kb, vb, ogs, pbuf, oin, qd, kv, dec, st,
                 *, seq, n_lb_rows, layer):
    C = CHUNK
    n_chunks = seq // C
    h = pl.program_id(1)

    def lower_bound(d):
        rows = [lbl_ref[d * n_lb_rows + r:d * n_lb_rows + r + 1, :] for r in range(n_lb_rows)]
        m = functools.reduce(jnp.maximum, rows)
        e = [jnp.exp(r - m) for r in rows]
        return sum(e[:layer + 1]) / sum(e)

    lb_f = lower_bound(0)
    lb_b = lower_bound(1)

    PR = 256

    def proj_body(r, carry):
        rows = pl.ds(pl.multiple_of(r * PR, PR), PR)
        pr = jnp.dot(u_ref[rows, :], w_ref[...], preferred_element_type=F32)
        q = pr[:, 0:128]
        qs[rows, :] = q * _sigmoid(q)
        f_f = lb_f + (1.0 - lb_f) * _sigmoid(pr[:, 128:256])
        gf[rows, :] = jnp.log(f_f)
        kf[rows, :] = 1.0 - f_f
        f_b = lb_b + (1.0 - lb_b) * _sigmoid(pr[:, 256:384])
        gb[rows, :] = jnp.log(f_b)
        kb[rows, :] = 1.0 - f_b
        vb[rows, :] = pr[:, 384:512].astype(BF16)
        og = pr[:, 512:640]
        ogs[rows, :] = og * _sigmoid(og)
        pbuf[pl.ds(pl.multiple_of(r * PR + C, C), PR), :] = pr[:, 640:768]
        return carry

    lax.fori_loop(0, seq // PR, proj_body, 0)
    pbuf[0:C, :] = jnp.zeros((C, HEAD_DIM), F32)
    pbuf[seq + C:seq + 2 * C, :] = jnp.zeros((C, HEAD_DIM), F32)

    def chunk_body(j, carry):
        rows = pl.ds(pl.multiple_of(j * C, C), C)
        row = lax.broadcasted_iota(jnp.int32, (C, C), 0)
        col = lax.broadcasted_iota(jnp.int32, (C, C), 1)
        rblk = row // BLOCK
        cblk = col // BLOCK
        odd = (rblk % 2) == 1
        upper = row >= C // 2
        cls_f = jnp.where((rblk == cblk) & (col <= row), 1,
                          jnp.where(odd & (cblk == rblk - 1), 2,
                                    jnp.where(upper & (col < C // 2), 3, 0)))
        cls_b = jnp.where((rblk == cblk) & (col >= row), 1,
                          jnp.where(jnp.logical_not(odd) & (cblk == rblk + 1), 2,
                                    jnp.where(jnp.logical_not(upper) & (col >= C // 2), 3, 0)))
        q = qs[rows, :]
        v = vb[rows, :]
        a_sum = None
        qd_parts, kend_parts, dec_parts = [], [], []
        for d, (g_ref, k_ref, cls, q_role32, q_role64) in enumerate((
                (gf, kf, cls_f, odd, upper),
                (gb, kb, cls_b, jnp.logical_not(odd), jnp.logical_not(upper)))):
            g = g_ref[rows, :]
            k = k_ref[rows, :]
            e2 = jnp.dot(wcum_ref[d], _split_hi_lo(g), preferred_element_type=F32)
            e = e2[:, 0:128] + e2[:, 128:256]
            e_c, e_32, e_64, e_pre, e_end = (e[i * C:(i + 1) * C] for i in range(N_EXP))
            q_c = (q * jnp.exp(e_c)).astype(BF16)
            k_c = (k * jnp.exp(-e_c)).astype(BF16)
            x_32 = (jnp.where(q_role32, q, k) * jnp.exp(e_32)).astype(BF16)
            x_64 = (jnp.where(q_role64, q, k) * jnp.exp(e_64)).astype(BF16)
            p_c = lax.dot_general(q_c, k_c, _NT, preferred_element_type=F32)
            p_32 = lax.dot_general(x_32, x_32, _NT, preferred_element_type=F32)
            p_64 = lax.dot_general(x_64, x_64, _NT, preferred_element_type=F32)
            a_d = jnp.where(cls == 1, p_c, jnp.where(cls == 2, p_32, jnp.where(cls == 3, p_64, 0.0)))
            a_sum = a_d if a_sum is None else a_sum + a_d
            qd_parts.append((q * jnp.exp(e_pre)).astype(BF16))
            kend_parts.append((k * jnp.exp(e_end)).astype(BF16))
            dec_parts.append(jnp.exp(jnp.sum(g, axis=0, keepdims=True)))
        oin[rows, :] = jnp.dot(a_sum.astype(BF16), v, preferred_element_type=F32)
        qd[rows, :] = jnp.concatenate(qd_parts, axis=1)
        kend = jnp.concatenate(kend_parts, axis=1)
        kv[j] = lax.dot_general(v, kend, _TN, preferred_element_type=F32)
        dec[j] = jnp.broadcast_to(jnp.concatenate(dec_parts, axis=1), (8, 2 * HEAD_DIM))
        return carry

    lax.fori_loop(0, n_chunks, chunk_body, 0)

    def fwd_body(j, s):
        st[j, :, 0:HEAD_DIM] = s.astype(BF16)
        return s * dec[j, 0:1, 0:HEAD_DIM] + kv[j, :, 0:HEAD_DIM]

    lax.fori_loop(0, n_chunks, fwd_body, jnp.zeros((HEAD_DIM, HEAD_DIM), F32))

    def bwd_body(i, s):
        j = n_chunks - 1 - i
        st[j, :, HEAD_DIM:2 * HEAD_DIM] = s.astype(BF16)
        return s * dec[j, 0:1, HEAD_DIM:2 * HEAD_DIM] + kv[j, :, HEAD_DIM:2 * HEAD_DIM]

    lax.fori_loop(0, n_chunks, bwd_body, jnp.zeros((HEAD_DIM, HEAD_DIM), F32))

    grp = h // (HEADS // len(POOL_WINDOWS))
    half = sum(jnp.where(grp == i, w // 2, 0) for i, w in enumerate(POOL_WINDOWS))
    t_loc = lax.broadcasted_iota(jnp.int32, (C, 3 * C), 0)
    s_ext = lax.broadcasted_iota(jnp.int32, (C, 3 * C), 1)
    off = s_ext - C - t_loc
    band = jnp.where((off >= 1 - half) & (off <= half), 1.0, 0.0).astype(BF16)
    norm_g = ng_ref[...]

    def out_body(j, carry):
        rows = pl.ds(pl.multiple_of(j * C, C), C)
        o = oin[rows, :] + lax.dot_general(qd[rows, :], st[j], _NT, preferred_element_type=F32)
        ms = jnp.mean(o * o, axis=-1, keepdims=True)
        y = o * lax.rsqrt(ms + RMS_EPS) * norm_g * ogs[rows, :]
        ya_ref[rows, :] = y.astype(ya_ref.dtype)
        p_ext = pbuf[pl.ds(pl.multiple_of(j * C, C), 3 * C), :]
        ws2 = jnp.dot(band, _split_hi_lo(p_ext), preferred_element_type=F32)
        ws = ws2[:, 0:128] + ws2[:, 128:256]
        t = j * C + lax.broadcasted_iota(jnp.int32, (C, HEAD_DIM), 0)
        cnt = jnp.minimum(t + half + 1, seq) - jnp.maximum(t - half + 1, 0)
        p_cur = pbuf[pl.ds(pl.multiple_of(j * C + C, C), C), :]
        yp_ref[rows, :] = (ws / cnt.astype(F32) - p_cur).astype(yp_ref.dtype)
        return carry

    lax.fori_loop(0, n_chunks, out_body, 0)


def _hgrn_pool(u3, w_heads, lb_rows, norm_g, wcum, layer):
    B, S, D = u3.shape
    n_lb_rows = lb_rows.shape[0] // 2
    n_chunks = S // CHUNK
    row_f32 = lambda: pltpu.VMEM((S, HEAD_DIM), F32)
    out_spec = pl.BlockSpec((None, S, HEAD_DIM), lambda b, h: (b, 0, h))
    return pl.pallas_call(
        functools.partial(_hgrn_kernel, seq=S, n_lb_rows=n_lb_rows, layer=layer),
        grid=(B, HEADS),
        in_specs=[
            pl.BlockSpec((None, S, D), lambda b, h: (b, 0, 0)),
            pl.BlockSpec((None, D, 6 * HEAD_DIM), lambda b, h: (h, 0, 0)),
            pl.BlockSpec((2 * n_lb_rows, HEAD_DIM), lambda b, h: (0, h)),
            pl.BlockSpec((1, HEAD_DIM), lambda b, h: (0, h)),
            pl.BlockSpec((2, N_EXP * CHUNK, CHUNK), lambda b, h: (0, 0, 0)),
        ],
        out_specs=[out_spec, out_spec],
        out_shape=[jax.ShapeDtypeStruct((B, S, D), BF16), jax.ShapeDtypeStruct((B, S, D), BF16)],
        scratch_shapes=[
            row_f32(), row_f32(), row_f32(), row_f32(), row_f32(),
            pltpu.VMEM((S, HEAD_DIM), BF16),
            row_f32(),
            pltpu.VMEM((S + 2 * CHUNK, HEAD_DIM), F32),
            row_f32(),
            pltpu.VMEM((S, 2 * HEAD_DIM), BF16),
            pltpu.VMEM((n_chunks, HEAD_DIM, 2 * HEAD_DIM), F32),
            pltpu.VMEM((n_chunks, 8, 2 * HEAD_DIM), F32),
            pltpu.VMEM((n_chunks, HEAD_DIM, 2 * HEAD_DIM), BF16),
        ],
        compiler_params=pltpu.CompilerParams(
            dimension_semantics=("arbitrary", "arbitrary"),
            vmem_limit_bytes=V7X_VMEM_LIMIT),
        name="hgrn_pool",
    )(u3, w_heads, lb_rows, norm_g, wcum)


def _merge_kernel(x_ref, u_ref, ya_ref, yp_ref, wg_ref, pw_ref, ps_ref, wa_ref, wb_ref, wo_ref, h_ref):
    D = x_ref.shape[1]
    n_grp = pw_ref.shape[0]
    gw = D // n_grp
    gates = jnp.dot(u_ref[...], wg_ref[...], preferred_element_type=F32)
    yb = jnp.concatenate(
        [jnp.dot(yp_ref[:, g * gw:(g + 1) * gw], pw_ref[g], preferred_element_type=F32)
         for g in range(n_grp)], axis=1)
    yb = (yb * ps_ref[...]).astype(BF16)
    za = jnp.dot(ya_ref[...], wa_ref[...], preferred_element_type=F32)
    zb = jnp.dot(yb, wb_ref[...], preferred_element_type=F32)
    merged = _sigmoid(gates[:, 0:D]) * za + _sigmoid(gates[:, D:2 * D]) * zb
    h_ref[...] = x_ref[...] + jnp.dot(merged.astype(BF16), wo_ref[...], preferred_element_type=F32)


def _const_spec(shape):
    return pl.BlockSpec(shape, lambda i: (0,) * len(shape), pipeline_mode=pl.Buffered(1))


def _merge(x2, u2, ya2, yp2, w_gates, pool_w, pool_scale, w_a, w_b, w_o, tile):
    T, D = x2.shape
    tok = lambda: pl.BlockSpec((tile, D), lambda i: (i, 0))
    return pl.pallas_call(
        _merge_kernel,
        grid=(T // tile,),
        in_specs=[tok(), tok(), tok(), tok(),
                  _const_spec(w_gates.shape), _const_spec(pool_w.shape), _const_spec((1, D)),
                  _const_spec(w_a.shape), _const_spec(w_b.shape), _const_spec(w_o.shape)],
        out_specs=tok(),
        out_shape=jax.ShapeDtypeStruct((T, D), F32),
        compiler_params=pltpu.CompilerParams(
            dimension_semantics=("arbitrary",), vmem_limit_bytes=V7X_VMEM_LIMIT),
        name="merge",
    )(x2, u2, ya2, yp2, w_gates, pool_w, pool_scale.reshape(1, D), w_a, w_b, w_o)


def _ffn_kernel(h_ref, g_ref, wi_ref, wo_ref, gfin_ref, o_ref, *, n_split, final_norm):
    h = h_ref[...]
    ms = jnp.mean(h * h, axis=-1, keepdims=True)
    u = (h * lax.rsqrt(ms + RMS_EPS) * g_ref[...]).astype(BF16)
    d_ff = wo_ref.shape[0]
    cw = d_ff // n_split
    acc = h
    for c in range(n_split):
        gate = jnp.dot(u, wi_ref[:, c * cw:(c + 1) * cw], preferred_element_type=F32)
        up = jnp.dot(u, wi_ref[:, d_ff + c * cw:d_ff + (c + 1) * cw], preferred_element_type=F32)
        act = (gate * _sigmoid(gate) * up).astype(BF16)
        acc = acc + jnp.dot(act, wo_ref[c * cw:(c + 1) * cw, :], preferred_element_type=F32)
    if final_norm:
        ms = jnp.mean(acc * acc, axis=-1, keepdims=True)
        acc = acc * lax.rsqrt(ms + RMS_EPS) * gfin_ref[...]
    o_ref[...] = acc


def _ffn(h2, g_ffn, w_in, w_out, g_final, tile, final_norm):
    T, D = h2.shape
    tok = lambda: pl.BlockSpec((tile, D), lambda i: (i, 0))
    return pl.pallas_call(
        functools.partial(_ffn_kernel, n_split=2, final_norm=final_norm),
        grid=(T // tile,),
        in_specs=[tok(), _const_spec((1, D)), _const_spec(w_in.shape), _const_spec(w_out.shape),
                  _const_spec((1, D))],
        out_specs=tok(),
        out_shape=jax.ShapeDtypeStruct((T, D), F32),
        compiler_params=pltpu.CompilerParams(
            dimension_semantics=("arbitrary",), vmem_limit_bytes=V7X_VMEM_LIMIT),
        name="ffn",
    )(h2, g_ffn.reshape(1, D), w_in, w_out, g_final.reshape(1, D))


def kernel(x, g_mix, w_in, lb_logits, hgrn_norm_g, pool_w, pool_scale, w_branch_a, w_branch_b,
           w_out, g_ffn, w_ffn_in, w_ffn_out, g_final):
    B, S, D = x.shape
    depth = w_in.shape[0]
    T = B * S
    assert D == HEADS * HEAD_DIM and S % 256 == 0
    n_head_cols = 6 * D
    wcum = jnp.asarray(_cumsum_matrices(), BF16)
    lb_rows = lb_logits.astype(F32).reshape(2 * (depth + 1), D)

    h2 = x.reshape(T, D)
    for l in range(depth):
        w_l = w_in[l].astype(BF16)
        w_heads = (w_l[:, :n_head_cols].reshape(D, 6, HEADS, HEAD_DIM)
                   .transpose(2, 0, 1, 3).reshape(HEADS, D, 6 * HEAD_DIM))
        w_gates = w_l[:, n_head_cols:]
        u2 = _rmsnorm(h2, g_mix[l], tile=1024)
        ya, yp = _hgrn_pool(u2.reshape(B, S, D), w_heads, lb_rows, hgrn_norm_g[l].reshape(1, D),
                            wcum, layer=l)
        h2 = _merge(h2, u2, ya.reshape(T, D), yp.reshape(T, D), w_gates,
                    pool_w[l].astype(BF16), pool_scale[l], w_branch_a[l].astype(BF16),
                    w_branch_b[l].astype(BF16), w_out[l].astype(BF16), tile=512)
        h2 = _ffn(h2, g_ffn[l], w_ffn_in[l].astype(BF16), w_ffn_out[l].astype(BF16), g_final,
                  tile=512, final_norm=(l == depth - 1))
    return h2.reshape(B, S, D)
```

```python
import functools
import math

import numpy as np
import jax
import jax.numpy as jnp
from jax import lax
from jax.experimental import pallas as pl
from jax.experimental.pallas import tpu as pltpu

F32 = jnp.float32
BF16 = jnp.bfloat16

RMS_EPS = 1e-6
HEADS = 8
HEAD_DIM = 128
POOL_WINDOWS = (2, 4, 8, 16)
CHUNK = 128
BLOCK = 32
LOG2_E = 1.0 / math.log(2.0)
V7X_VMEM_LIMIT = 56 * 1024 * 1024

_NT = (((1,), (1,)), ((), ()))
_TN = (((0,), (0,)), ((), ()))


def _sigmoid(x):
    return 1.0 / (1.0 + jnp.exp(-x))


def _chunk_constants():
    t = np.arange(CHUNK)[:, None]
    s = np.arange(CHUNK)[None, :]
    tri = np.stack([s <= t, s >= t]).astype(np.float32)
    same = (t // BLOCK) == (s // BLOCK)
    dmask = np.stack([same & (s <= t), same & (s >= t)]).astype(np.int32)
    return tri, dmask


def _split_hi_lo(a):
    hi = a.astype(BF16)
    lo = (a - hi.astype(F32)).astype(BF16)
    return jnp.concatenate([hi, lo], axis=1)


def _rmsnorm_kernel(x_ref, g_ref, o_ref):
    x = x_ref[...]
    ms = jnp.mean(x * x, axis=-1, keepdims=True)
    o_ref[...] = (x * lax.rsqrt(ms + RMS_EPS) * g_ref[...]).astype(o_ref.dtype)


def _rmsnorm(x2, g, tile):
    T, D = x2.shape
    return pl.pallas_call(
        _rmsnorm_kernel,
        grid=(T // tile,),
        in_specs=[pl.BlockSpec((tile, D), lambda i: (i, 0)),
                  pl.BlockSpec((1, D), lambda i: (0, 0))],
        out_specs=pl.BlockSpec((tile, D), lambda i: (i, 0)),
        out_shape=jax.ShapeDtypeStruct((T, D), BF16),
        compiler_params=pltpu.CompilerParams(dimension_semantics=("arbitrary",)),
        name="rmsnorm",
    )(x2, g.reshape(1, D))


def _hgrn_kernel(u_ref, w_ref, lbl_ref, ng_ref, tri_ref, dmask_ref, ya_ref, yp_ref,
                 qs, gf, gb, kf, kb, vb, ogs, pf, pbuf, pre_s, qc_s, kc_s, x32_s, x64_s, kend_s,
                 a_s, qd, kv, dec, st,
                 *, seq, n_lb_rows, layer):
    C = CHUNK
    n_chunks = seq // C
    h = pl.program_id(1)

    def lower_bound(d):
        rows = [lbl_ref[d * n_lb_rows + r:d * n_lb_rows + r + 1, :] for r in range(n_lb_rows)]
        m = functools.reduce(jnp.maximum, rows)
        e = [jnp.exp(r - m) for r in rows]
        return sum(e[:layer + 1]) / sum(e)

    lb_f = lower_bound(0)
    lb_b = lower_bound(1)

    PR = 512
    norm_g = ng_ref[...]

    def proj_body(r, carry):
        rows = pl.ds(pl.multiple_of(r * PR, PR), PR)
        pr = jnp.dot(u_ref[rows, :], w_ref[...], preferred_element_type=F32)
        q = pr[:, 0:128]
        qs[rows, :] = q * _sigmoid(q)
        f_f = lb_f + (1.0 - lb_f) * _sigmoid(pr[:, 128:256])
        gf[rows, :] = jnp.log(f_f) * LOG2_E
        kf[rows, :] = 1.0 - f_f
        f_b = lb_b + (1.0 - lb_b) * _sigmoid(pr[:, 256:384])
        gb[rows, :] = jnp.log(f_b) * LOG2_E
        kb[rows, :] = 1.0 - f_b
        vb[rows, :] = pr[:, 384:512].astype(BF16)
        og = pr[:, 512:640]
        ogs[rows, :] = og * _sigmoid(og) * norm_g
        p = pr[:, 640:768]
        pf[rows, :] = p
        pbuf[pl.ds(pl.multiple_of(r * PR + C, C), PR), :] = _split_hi_lo(p)
        return carry

    lax.fori_loop(0, seq // PR, proj_body, 0)
    pbuf[0:C, :] = jnp.zeros((C, 2 * HEAD_DIM), BF16)
    pbuf[seq + C:seq + 2 * C, :] = jnp.zeros((C, 2 * HEAD_DIM), BF16)

    n_blk = C // BLOCK
    zero_blk = jnp.zeros((BLOCK, HEAD_DIM), BF16)

    def place(blocks):
        return jnp.concatenate([blocks.get(b, zero_blk) for b in range(n_blk)], axis=0)

    def prefix_body(j, carry):
        rows = pl.ds(pl.multiple_of(j * C, C), C)
        for d, g_ref in enumerate((gf, gb)):
            p2 = jnp.dot(tri_ref[d], _split_hi_lo(g_ref[rows, :]), preferred_element_type=F32)
            pre_s[d, j] = p2[:, 0:128] + p2[:, 128:256]
        return carry

    lax.fori_loop(0, n_chunks, prefix_body, 0, unroll=4)

    def operand_body(j, carry):
        rows = pl.ds(pl.multiple_of(j * C, C), C)
        q = qs[rows, :]
        qd_parts, kend_parts, dec_parts = [], [], []
        for d, k_ref in enumerate((kf, kb)):
            fwd = d == 0
            k = k_ref[rows, :]
            pre = pre_s[d, j]

            def anchor(r, d=d):
                return pre_s[d, j, r:r + 1, :]

            a_64 = anchor(C // 2 - 1 if fwd else C // 2)
            a_end = anchor(C - 1 if fwd else 0)
            qc, kc, x32, x64, qdb, keb = [], [], [], [], [], []
            for b in range(n_blk):
                lo = b * BLOCK
                pre_b, q_b, k_b = pre[lo:lo + BLOCK], q[lo:lo + BLOCK], k[lo:lo + BLOCK]
                mid = anchor(lo + BLOCK // 2 - 1 if fwd else lo + BLOCK // 2)
                e_mid = pre_b - mid
                qc_b = q_b * jnp.exp2(e_mid)
                kc_b = k_b * jnp.exp2(-e_mid)
                qc.append(qc_b)
                kc.append(kc_b)
                if (b % 2 == 1) == fwd:
                    x32.append(qc_b * jnp.exp2(mid - anchor(lo - 1 if fwd else lo + BLOCK)))
                else:
                    x32.append(kc_b * jnp.exp2(anchor(lo + BLOCK - 1 if fwd else lo) - mid))
                if (b >= n_blk // 2) == fwd:
                    x64.append(qc_b * jnp.exp2(mid - a_64))
                else:
                    x64.append(kc_b * jnp.exp2(a_64 - mid))
                qdb.append(qc_b * jnp.exp2(mid))
                keb.append(kc_b * jnp.exp2(a_end - mid))
            qc_s[d, rows, :] = jnp.concatenate(qc, axis=0).astype(BF16)
            kc_s[d, rows, :] = jnp.concatenate(kc, axis=0).astype(BF16)
            x32_s[d, rows, :] = jnp.concatenate(x32, axis=0).astype(BF16)
            x64_s[d, rows, :] = jnp.concatenate(x64, axis=0).astype(BF16)
            qd_parts.append(jnp.concatenate(qdb, axis=0).astype(BF16))
            kend_parts.append(jnp.concatenate(keb, axis=0).astype(BF16))
            dec_parts.append(jnp.exp2(a_end))
        qd[rows, :] = jnp.concatenate(qd_parts, axis=1)
        kend_s[rows, :] = jnp.concatenate(kend_parts, axis=1)
        dec[j] = jnp.broadcast_to(jnp.concatenate(dec_parts, axis=1), (8, 2 * HEAD_DIM))
        return carry

    lax.fori_loop(0, n_chunks, operand_body, 0, unroll=2)

    def chunk_body(j, carry):
        rows = pl.ds(pl.multiple_of(j * C, C), C)
        lhs_off, rhs_off = [], []
        a_diag = None
        for d in range(2):
            fwd = d == 0
            p_d = lax.dot_general(qc_s[d, rows, :], kc_s[d, rows, :], _NT, preferred_element_type=F32)
            a_d = jnp.where(dmask_ref[d] != 0, p_d, 0.0)
            a_diag = a_d if a_diag is None else a_diag + a_d
            x32 = x32_s[d, rows, :]
            x64 = x64_s[d, rows, :]
            blk = lambda x, b: x[b * BLOCK:(b + 1) * BLOCK]
            for pair in range(n_blk // 2):
                bq, bk = (2 * pair + 1, 2 * pair) if fwd else (2 * pair, 2 * pair + 1)
                lhs_off.append(place({bq: blk(x32, bq)}))
                rhs_off.append(place({bk: blk(x32, bk)}))
            q_half = range(n_blk // 2, n_blk) if fwd else range(n_blk // 2)
            k_half = range(n_blk // 2) if fwd else range(n_blk // 2, n_blk)
            lhs_off.append(place({b: blk(x64, b) for b in q_half}))
            rhs_off.append(place({b: blk(x64, b) for b in k_half}))
        a_off = lax.dot_general(jnp.concatenate(lhs_off, axis=1), jnp.concatenate(rhs_off, axis=1),
                                _NT, preferred_element_type=F32)
        a_s[rows, :] = (a_diag + a_off).astype(BF16)
        kv[j] = lax.dot_general(vb[rows, :], kend_s[rows, :], _TN, preferred_element_type=F32)
        return carry

    lax.fori_loop(0, n_chunks, chunk_body, 0, unroll=4)

    def fwd_body(j, s):
        st[j, :, 0:HEAD_DIM] = s.astype(BF16)
        return s * dec[j, 0:1, 0:HEAD_DIM] + kv[j, :, 0:HEAD_DIM]

    lax.fori_loop(0, n_chunks, fwd_body, jnp.zeros((HEAD_DIM, HEAD_DIM), F32))

    def bwd_body(i, s):
        j = n_chunks - 1 - i
        st[j, :, HEAD_DIM:2 * HEAD_DIM] = s.astype(BF16)
        return s * dec[j, 0:1, HEAD_DIM:2 * HEAD_DIM] + kv[j, :, HEAD_DIM:2 * HEAD_DIM]

    lax.fori_loop(0, n_chunks, bwd_body, jnp.zeros((HEAD_DIM, HEAD_DIM), F32))

    grp = h // (HEADS // len(POOL_WINDOWS))
    half = sum(jnp.where(grp == i, w // 2, 0) for i, w in enumerate(POOL_WINDOWS))
    t_loc = lax.broadcasted_iota(jnp.int32, (C, 3 * C), 0)
    s_ext = lax.broadcasted_iota(jnp.int32, (C, 3 * C), 1)
    off = s_ext - C - t_loc
    band = jnp.where((off >= 1 - half) & (off <= half), 1.0, 0.0).astype(BF16)
    inv_w = sum(jnp.where(grp == i, 1.0 / w, 0.0) for i, w in enumerate(POOL_WINDOWS))

    def out_chunk(j, clipped):
        start = j * C if isinstance(j, int) else pl.multiple_of(j * C, C)
        rows = pl.ds(start, C)
        o = (jnp.dot(a_s[rows, :], vb[rows, :], preferred_element_type=F32)
             + lax.dot_general(qd[rows, :], st[j], _NT, preferred_element_type=F32))
        ms = jnp.mean(o * o, axis=-1, keepdims=True)
        ya_ref[rows, :] = (o * lax.rsqrt(ms + RMS_EPS) * ogs[rows, :]).astype(ya_ref.dtype)
        ws2 = jnp.dot(band, pbuf[pl.ds(start, 3 * C), :], preferred_element_type=F32)
        ws = ws2[:, 0:128] + ws2[:, 128:256]
        if clipped:
            t = j * C + lax.broadcasted_iota(jnp.int32, (C, HEAD_DIM), 0)
            cnt = jnp.minimum(t + half + 1, seq) - jnp.maximum(t - half + 1, 0)
            mean = ws / cnt.astype(F32)
        else:
            mean = ws * inv_w
        yp_ref[rows, :] = (mean - pf[rows, :]).astype(yp_ref.dtype)

    edge_chunks = sorted({0, n_chunks - 1})
    for j in edge_chunks:
        out_chunk(j, clipped=True)

    def out_body(j, carry):
        out_chunk(j, clipped=False)
        return carry

    lax.fori_loop(1, n_chunks - 1, out_body, 0, unroll=7)


def _hgrn_pool(u3, w_heads, lb_rows, norm_g, tri, dmask, layer):
    B, S, D = u3.shape
    n_lb_rows = lb_rows.shape[0] // 2
    n_chunks = S // CHUNK
    row_f32 = lambda: pltpu.VMEM((S, HEAD_DIM), F32)
    dir_bf16 = lambda: pltpu.VMEM((2, S, HEAD_DIM), BF16)
    out_spec = pl.BlockSpec((None, S, HEAD_DIM), lambda b, h: (b, 0, h))
    return pl.pallas_call(
        functools.partial(_hgrn_kernel, seq=S, n_lb_rows=n_lb_rows, layer=layer),
        grid=(B, HEADS),
        in_specs=[
            pl.BlockSpec((None, S, D), lambda b, h: (b, 0, 0)),
            pl.BlockSpec((None, D, 6 * HEAD_DIM), lambda b, h: (h, 0, 0)),
            pl.BlockSpec((2 * n_lb_rows, HEAD_DIM), lambda b, h: (0, h)),
            pl.BlockSpec((1, HEAD_DIM), lambda b, h: (0, h)),
            pl.BlockSpec((2, CHUNK, CHUNK), lambda b, h: (0, 0, 0)),
            pl.BlockSpec((2, CHUNK, CHUNK), lambda b, h: (0, 0, 0)),
        ],
        out_specs=[out_spec, out_spec],
        out_shape=[jax.ShapeDtypeStruct((B, S, D), BF16), jax.ShapeDtypeStruct((B, S, D), BF16)],
        scratch_shapes=[
            row_f32(), row_f32(), row_f32(), row_f32(), row_f32(),
            pltpu.VMEM((S, HEAD_DIM), BF16),
            row_f32(),
            row_f32(),
            pltpu.VMEM((S + 2 * CHUNK, 2 * HEAD_DIM), BF16),
            pltpu.VMEM((2, n_chunks, CHUNK, HEAD_DIM), F32),
            dir_bf16(), dir_bf16(), dir_bf16(), dir_bf16(),
            pltpu.VMEM((S, 2 * HEAD_DIM), BF16),
            pltpu.VMEM((S, CHUNK), BF16),
            pltpu.VMEM((S, 2 * HEAD_DIM), BF16),
            pltpu.VMEM((n_chunks, HEAD_DIM, 2 * HEAD_DIM), F32),
            pltpu.VMEM((n_chunks, 8, 2 * HEAD_DIM), F32),
            pltpu.VMEM((n_chunks, HEAD_DIM, 2 * HEAD_DIM), BF16),
        ],
        compiler_params=pltpu.CompilerParams(
            dimension_semantics=("arbitrary", "arbitrary"),
            vmem_limit_bytes=V7X_VMEM_LIMIT),
        name="hgrn_pool",
    )(u3, w_heads, lb_rows, norm_g, tri, dmask)


def _merge_kernel(x_ref, u_ref, ya_ref, yp_ref, wg_ref, pw_ref, ps_ref, wa_ref, wb_ref, wo_ref, h_ref):
    D = x_ref.shape[1]
    n_grp = pw_ref.shape[0]
    gw = D // n_grp
    gates = jnp.dot(u_ref[...], wg_ref[...], preferred_element_type=F32)
    yb = jnp.concatenate(
        [jnp.dot(yp_ref[:, g * gw:(g + 1) * gw], pw_ref[g], preferred_element_type=F32)
         for g in range(n_grp)], axis=1)
    yb = (yb * ps_ref[...]).astype(BF16)
    za = jnp.dot(ya_ref[...], wa_ref[...], preferred_element_type=F32)
    zb = jnp.dot(yb, wb_ref[...], preferred_element_type=F32)
    merged = _sigmoid(gates[:, 0:D]) * za + _sigmoid(gates[:, D:2 * D]) * zb
    h_ref[...] = x_ref[...] + jnp.dot(merged.astype(BF16), wo_ref[...], preferred_element_type=F32)


def _const_spec(shape):
    return pl.BlockSpec(shape, lambda i: (0,) * len(shape), pipeline_mode=pl.Buffered(1))


def _merge(x2, u2, ya2, yp2, w_gates, pool_w, pool_scale, w_a, w_b, w_o, tile):
    T, D = x2.shape
    tok = lambda: pl.BlockSpec((tile, D), lambda i: (i, 0))
    return pl.pallas_call(
        _merge_kernel,
        grid=(T // tile,),
        in_specs=[tok(), tok(), tok(), tok(),
                  _const_spec(w_gates.shape), _const_spec(pool_w.shape), _const_spec((1, D)),
                  _const_spec(w_a.shape), _const_spec(w_b.shape), _const_spec(w_o.shape)],
        out_specs=tok(),
        out_shape=jax.ShapeDtypeStruct((T, D), F32),
        compiler_params=pltpu.CompilerParams(
            dimension_semantics=("arbitrary",), vmem_limit_bytes=V7X_VMEM_LIMIT),
        name="merge",
    )(x2, u2, ya2, yp2, w_gates, pool_w, pool_scale.reshape(1, D), w_a, w_b, w_o)


def _ffn_kernel(h_ref, g_ref, wi_ref, wo_ref, gfin_ref, o_ref, *, n_split, final_norm):
    h = h_ref[...]
    ms = jnp.mean(h * h, axis=-1, keepdims=True)
    u = (h * lax.rsqrt(ms + RMS_EPS) * g_ref[...]).astype(BF16)
    d_ff = wo_ref.shape[0]
    cw = d_ff // n_split
    acc = h
    for c in range(n_split):
        gate = jnp.dot(u, wi_ref[:, c * cw:(c + 1) * cw], preferred_element_type=F32)
        up = jnp.dot(u, wi_ref[:, d_ff + c * cw:d_ff + (c + 1) * cw], preferred_element_type=F32)
        act = (gate * _sigmoid(gate) * up).astype(BF16)
        acc = acc + jnp.dot(act, wo_ref[c * cw:(c + 1) * cw, :], preferred_element_type=F32)
    if final_norm:
        ms = jnp.mean(acc * acc, axis=-1, keepdims=True)
        acc = acc * lax.rsqrt(ms + RMS_EPS) * gfin_ref[...]
    o_ref[...] = acc


def _ffn(h2, g_ffn, w_in, w_out, g_final, tile, final_norm):
    T, D = h2.shape
    tok = lambda: pl.BlockSpec((tile, D), lambda i: (i, 0))
    return pl.pallas_call(
        functools.partial(_ffn_kernel, n_split=2, final_norm=final_norm),
        grid=(T // tile,),
        in_specs=[tok(), _const_spec((1, D)), _const_spec(w_in.shape), _const_spec(w_out.shape),
                  _const_spec((1, D))],
        out_specs=tok(),
        out_shape=jax.ShapeDtypeStruct((T, D), F32),
        compiler_params=pltpu.CompilerParams(
            dimension_semantics=("arbitrary",), vmem_limit_bytes=V7X_VMEM_LIMIT),
        name="ffn",
    )(h2, g_ffn.reshape(1, D), w_in, w_out, g_final.reshape(1, D))


def kernel(x, g_mix, w_in, lb_logits, hgrn_norm_g, pool_w, pool_scale, w_branch_a, w_branch_b,
           w_out, g_ffn, w_ffn_in, w_ffn_out, g_final):
    B, S, D = x.shape
    depth = w_in.shape[0]
    T = B * S
    assert D == HEADS * HEAD_DIM and S % 256 == 0
    n_head_cols = 6 * D
    tri_np, dmask_np = _chunk_constants()
    tri = jnp.asarray(tri_np, BF16)
    dmask = jnp.asarray(dmask_np, jnp.int32)
    lb_rows = lb_logits.astype(F32).reshape(2 * (depth + 1), D)

    h2 = x.reshape(T, D)
    for l in range(depth):
        w_l = w_in[l].astype(BF16)
        w_heads = (w_l[:, :n_head_cols].reshape(D, 6, HEADS, HEAD_DIM)
                   .transpose(2, 0, 1, 3).reshape(HEADS, D, 6 * HEAD_DIM))
        w_gates = w_l[:, n_head_cols:]
        u2 = _rmsnorm(h2, g_mix[l], tile=1024)
        ya, yp = _hgrn_pool(u2.reshape(B, S, D), w_heads, lb_rows, hgrn_norm_g[l].reshape(1, D),
                            tri, dmask, layer=l)
        h2 = _merge(h2, u2, ya.reshape(T, D), yp.reshape(T, D), w_gates,
                    pool_w[l].astype(BF16), pool_scale[l], w_branch_a[l].astype(BF16),
                    w_branch_b[l].astype(BF16), w_out[l].astype(BF16), tile=512)
        h2 = _ffn(h2, g_ffn[l], w_ffn_in[l].astype(BF16), w_ffn_out[l].astype(BF16), g_final,
                  tile=512, final_norm=(l == depth - 1))
    return h2.reshape(B, S, D)
```

```python
import functools
import math

import numpy as np
import jax
import jax.numpy as jnp
from jax import lax
from jax.experimental import pallas as pl
from jax.experimental.pallas import tpu as pltpu

F32 = jnp.float32
BF16 = jnp.bfloat16

RMS_EPS = 1e-6
HEADS = 8
HEAD_DIM = 128
POOL_WINDOWS = (2, 4, 8, 16)
CHUNK = 128
BLOCK = 32
HALO = 16
LOG2_E = 1.0 / math.log(2.0)
V7X_VMEM_LIMIT = 56 * 1024 * 1024

_NT = (((1,), (1,)), ((), ()))
_TN = (((0,), (0,)), ((), ()))


def _sigmoid(x):
    return 1.0 / (1.0 + jnp.exp(-x))


def _chunk_constants():
    t = np.arange(CHUNK)[:, None]
    s = np.arange(CHUNK)[None, :]
    tri = np.zeros((2 * CHUNK, 2 * CHUNK), np.float32)
    tri[:CHUNK, :CHUNK] = s <= t
    tri[CHUNK:, CHUNK:] = s >= t
    same = (t // BLOCK) == (s // BLOCK)
    cross = (t >= CHUNK // 2) != (s >= CHUNK // 2)
    dmask = np.stack([same & (s <= t), same & (s >= t), cross]).astype(np.int32)
    return tri, dmask


def _split_hi_lo(a):
    hi = a.astype(BF16)
    lo = (a - hi.astype(F32)).astype(BF16)
    return jnp.concatenate([hi, lo], axis=1)


def _rmsnorm_kernel(x_ref, g_ref, o_ref):
    x = x_ref[...]
    ms = jnp.mean(x * x, axis=-1, keepdims=True)
    o_ref[...] = (x * lax.rsqrt(ms + RMS_EPS) * g_ref[...]).astype(o_ref.dtype)


def _rmsnorm(x2, g, tile):
    T, D = x2.shape
    return pl.pallas_call(
        _rmsnorm_kernel,
        grid=(T // tile,),
        in_specs=[pl.BlockSpec((tile, D), lambda i: (i, 0)),
                  pl.BlockSpec((1, D), lambda i: (0, 0))],
        out_specs=pl.BlockSpec((tile, D), lambda i: (i, 0)),
        out_shape=jax.ShapeDtypeStruct((T, D), BF16),
        compiler_params=pltpu.CompilerParams(dimension_semantics=("arbitrary",)),
        name="rmsnorm",
    )(x2, g.reshape(1, D))


def _hgrn_kernel(u_ref, w_ref, lbl_ref, ng_ref, tri_ref, dmask_ref, ya_ref, yp_ref,
                 qs, kf, kb, g_s, vb, ogs, pf, pbuf, pre_s, qc_s, kc_s, x32_s, x64q_s, x64k_s, kend_s,
                 a_s, qd, kv, dec, st,
                 *, seq, n_lb_rows, layer):
    C = CHUNK
    n_chunks = seq // C
    h = pl.program_id(1)

    def lower_bound(d):
        rows = [lbl_ref[d * n_lb_rows + r:d * n_lb_rows + r + 1, :] for r in range(n_lb_rows)]
        m = functools.reduce(jnp.maximum, rows)
        e = [jnp.exp(r - m) for r in rows]
        return sum(e[:layer + 1]) / sum(e)

    lb_f = lower_bound(0)
    lb_b = lower_bound(1)

    PR = 512
    norm_g = ng_ref[...]

    def proj_body(r, carry):
        rows = pl.ds(pl.multiple_of(r * PR, PR), PR)
        pr = jnp.dot(u_ref[rows, :], w_ref[...], preferred_element_type=F32)
        q = pr[:, 0:128]
        qs[rows, :] = q * _sigmoid(q)
        f_f = lb_f + (1.0 - lb_f) * _sigmoid(pr[:, 128:256])
        g_f = _split_hi_lo(jnp.log(f_f) * LOG2_E)
        kf[rows, :] = 1.0 - f_f
        f_b = lb_b + (1.0 - lb_b) * _sigmoid(pr[:, 256:384])
        g_b = _split_hi_lo(jnp.log(f_b) * LOG2_E)
        for c in range(PR // C):
            g_s[r * (PR // C) + c, 0:C, :] = g_f[c * C:(c + 1) * C]
            g_s[r * (PR // C) + c, C:2 * C, :] = g_b[c * C:(c + 1) * C]
        kb[rows, :] = 1.0 - f_b
        vb[rows, :] = pr[:, 384:512].astype(BF16)
        og = pr[:, 512:640]
        ogs[rows, :] = og * _sigmoid(og) * norm_g
        p = pr[:, 640:768]
        pf[rows, :] = p
        pbuf[pl.ds(pl.multiple_of(r * PR + C, C), PR), :] = _split_hi_lo(p)
        return carry

    lax.fori_loop(0, seq // PR, proj_body, 0)
    pbuf[0:C, :] = jnp.zeros((C, 2 * HEAD_DIM), BF16)
    pbuf[seq + C:seq + 2 * C, :] = jnp.zeros((C, 2 * HEAD_DIM), BF16)

    n_blk = C // BLOCK
    zero_blk = jnp.zeros((BLOCK, HEAD_DIM), BF16)

    def place(blocks):
        return jnp.concatenate([blocks.get(b, zero_blk) for b in range(n_blk)], axis=0)

    def prefix_body(j, carry):
        p2 = jnp.dot(tri_ref[...], g_s[j], preferred_element_type=F32)
        pre = p2[:, 0:128] + p2[:, 128:256]
        pre_s[0, j] = pre[0:C]
        pre_s[1, j] = pre[C:2 * C]
        return carry

    lax.fori_loop(0, n_chunks, prefix_body, 0, unroll=8)

    def operand_body(j, carry):
        rows = pl.ds(pl.multiple_of(j * C, C), C)
        q = qs[rows, :]
        qd_parts, kend_parts, dec_parts = [], [], []
        q64, k64 = {}, {}
        for d, k_ref in enumerate((kf, kb)):
            fwd = d == 0
            k = k_ref[rows, :]
            pre = pre_s[d, j]

            def anchor(r, d=d):
                return pre_s[d, j, r:r + 1, :]

            a_64 = anchor(C // 2 - 1 if fwd else C // 2)
            a_end = anchor(C - 1 if fwd else 0)
            qc, kc, x32, qdb, keb = [], [], [], [], []
            for b in range(n_blk):
                lo = b * BLOCK
                pre_b, q_b, k_b = pre[lo:lo + BLOCK], q[lo:lo + BLOCK], k[lo:lo + BLOCK]
                mid = anchor(lo + BLOCK // 2 - 1 if fwd else lo + BLOCK // 2)
                e_mid = pre_b - mid
                qc_b = (q_b * jnp.exp2(e_mid)).astype(BF16)
                kc_b = (k_b * jnp.exp2(-e_mid)).astype(BF16)
                qc.append(qc_b)
                kc.append(kc_b)

                def factor(e):
                    return jnp.broadcast_to(jnp.exp2(e), (BLOCK, HEAD_DIM)).astype(BF16)

                if (b % 2 == 1) == fwd:
                    x32.append(qc_b * factor(mid - anchor(lo - 1 if fwd else lo + BLOCK)))
                else:
                    x32.append(kc_b * factor(anchor(lo + BLOCK - 1 if fwd else lo) - mid))
                if (b >= n_blk // 2) == fwd:
                    q64[b] = qc_b * factor(mid - a_64)
                else:
                    k64[b] = kc_b * factor(a_64 - mid)
                qdb.append(qc_b * factor(mid))
                keb.append(kc_b * factor(a_end - mid))
            qc_s[d, rows, :] = jnp.concatenate(qc, axis=0)
            kc_s[d, rows, :] = jnp.concatenate(kc, axis=0)
            x32_s[d, rows, :] = jnp.concatenate(x32, axis=0)
            qd_parts.append(jnp.concatenate(qdb, axis=0))
            kend_parts.append(jnp.concatenate(keb, axis=0))
            dec_parts.append(jnp.exp2(a_end))
        x64q_s[rows, :] = jnp.concatenate([q64[b] for b in range(n_blk)], axis=0)
        x64k_s[rows, :] = jnp.concatenate([k64[b] for b in range(n_blk)], axis=0)
        qd[rows, :] = jnp.concatenate(qd_parts, axis=1)
        kend_s[rows, :] = jnp.concatenate(kend_parts, axis=1)
        dec[j] = jnp.broadcast_to(jnp.concatenate(dec_parts, axis=1), (8, 2 * HEAD_DIM))
        return carry

    def chunk_body(j, carry):
        rows = pl.ds(pl.multiple_of(j * C, C), C)
        lhs32, rhs32 = [], []
        a_near = None
        for d in range(2):
            p_d = lax.dot_general(qc_s[d, rows, :], kc_s[d, rows, :], _NT, preferred_element_type=F32)
            a_d = jnp.where(dmask_ref[d] != 0, p_d, 0.0)
            a_near = a_d if a_near is None else a_near + a_d
            x32 = x32_s[d, rows, :]
            blocks = {b: x32[b * BLOCK:(b + 1) * BLOCK] for b in range(n_blk)}
            q_blocks = [b for b in range(n_blk) if (b % 2 == 1) == (d == 0)]
            lhs32.append(place({b: x for b, x in blocks.items() if b in q_blocks}))
            rhs32.append(place({b: x for b, x in blocks.items() if b not in q_blocks}))
        p_32 = lax.dot_general(jnp.concatenate(lhs32, axis=1), jnp.concatenate(rhs32, axis=1),
                               _NT, preferred_element_type=F32)
        p_64 = lax.dot_general(x64q_s[rows, :], x64k_s[rows, :], _NT, preferred_element_type=F32)
        a_s[rows, :] = jnp.where(dmask_ref[2] != 0, p_64, a_near + p_32).astype(BF16)
        kv[j] = lax.dot_general(vb[rows, :], kend_s[rows, :], _TN, preferred_element_type=F32)
        return carry

    def fused_body(j, carry):
        return chunk_body(j, operand_body(j, carry))

    lax.fori_loop(0, n_chunks, fused_body, 0, unroll=4)

    def fwd_body(j, s):
        st[j, :, 0:HEAD_DIM] = s.astype(BF16)
        return s * dec[j, 0:1, 0:HEAD_DIM] + kv[j, :, 0:HEAD_DIM]

    lax.fori_loop(0, n_chunks, fwd_body, jnp.zeros((HEAD_DIM, HEAD_DIM), F32))

    def bwd_body(i, s):
        j = n_chunks - 1 - i
        st[j, :, HEAD_DIM:2 * HEAD_DIM] = s.astype(BF16)
        return s * dec[j, 0:1, HEAD_DIM:2 * HEAD_DIM] + kv[j, :, HEAD_DIM:2 * HEAD_DIM]

    lax.fori_loop(0, n_chunks, bwd_body, jnp.zeros((HEAD_DIM, HEAD_DIM), F32))

    grp = h // (HEADS // len(POOL_WINDOWS))
    half = sum(jnp.where(grp == i, w // 2, 0) for i, w in enumerate(POOL_WINDOWS))
    t_loc = lax.broadcasted_iota(jnp.int32, (C, C + 2 * HALO), 0)
    s_ext = lax.broadcasted_iota(jnp.int32, (C, C + 2 * HALO), 1)
    off = s_ext - HALO - t_loc
    band = jnp.where((off >= 1 - half) & (off <= half), 1.0, 0.0).astype(BF16)
    inv_w = sum(jnp.where(grp == i, 1.0 / w, 0.0) for i, w in enumerate(POOL_WINDOWS))

    def out_chunk(j, clipped):
        start = j * C if isinstance(j, int) else pl.multiple_of(j * C, C)
        rows = pl.ds(start, C)
        o = (jnp.dot(a_s[rows, :], vb[rows, :], preferred_element_type=F32)
             + lax.dot_general(qd[rows, :], st[j], _NT, preferred_element_type=F32))
        ms = jnp.mean(o * o, axis=-1, keepdims=True)
        ya_ref[rows, :] = (o * lax.rsqrt(ms + RMS_EPS) * ogs[rows, :]).astype(ya_ref.dtype)
        p_start = start + C - HALO
        if not isinstance(j, int):
            p_start = pl.multiple_of(p_start, HALO)
        ws2 = jnp.dot(band, pbuf[pl.ds(p_start, C + 2 * HALO), :], preferred_element_type=F32)
        ws = ws2[:, 0:128] + ws2[:, 128:256]
        if clipped:
            t = j * C + lax.broadcasted_iota(jnp.int32, (C, HEAD_DIM), 0)
            cnt = jnp.minimum(t + half + 1, seq) - jnp.maximum(t - half + 1, 0)
            mean = ws / cnt.astype(F32)
        else:
            mean = ws * inv_w
        yp_ref[rows, :] = (mean - pf[rows, :]).astype(yp_ref.dtype)

    edge_chunks = sorted({0, n_chunks - 1})
    for j in edge_chunks:
        out_chunk(j, clipped=True)

    def out_body(j, carry):
        out_chunk(j, clipped=False)
        return carry

    lax.fori_loop(1, n_chunks - 1, out_body, 0, unroll=7)


def _hgrn_pool(u3, w_heads, lb_rows, norm_g, tri, dmask, layer):
    B, S, D = u3.shape
    n_lb_rows = lb_rows.shape[0] // 2
    n_chunks = S // CHUNK
    row_f32 = lambda: pltpu.VMEM((S, HEAD_DIM), F32)
    dir_bf16 = lambda: pltpu.VMEM((2, S, HEAD_DIM), BF16)
    out_spec = pl.BlockSpec((None, S, HEAD_DIM), lambda b, h: (b, 0, h))
    return pl.pallas_call(
        functools.partial(_hgrn_kernel, seq=S, n_lb_rows=n_lb_rows, layer=layer),
        grid=(B, HEADS),
        in_specs=[
            pl.BlockSpec((None, S, D), lambda b, h: (b, 0, 0)),
            pl.BlockSpec((None, D, 6 * HEAD_DIM), lambda b, h: (h, 0, 0)),
            pl.BlockSpec((2 * n_lb_rows, HEAD_DIM), lambda b, h: (0, h)),
            pl.BlockSpec((1, HEAD_DIM), lambda b, h: (0, h)),
            pl.BlockSpec((2 * CHUNK, 2 * CHUNK), lambda b, h: (0, 0)),
            pl.BlockSpec((3, CHUNK, CHUNK), lambda b, h: (0, 0, 0)),
        ],
        out_specs=[out_spec, out_spec],
        out_shape=[jax.ShapeDtypeStruct((B, S, D), BF16), jax.ShapeDtypeStruct((B, S, D), BF16)],
        scratch_shapes=[
            row_f32(), row_f32(), row_f32(),
            pltpu.VMEM((n_chunks, 2 * CHUNK, 2 * HEAD_DIM), BF16),
            pltpu.VMEM((S, HEAD_DIM), BF16),
            row_f32(),
            row_f32(),
            pltpu.VMEM((S + 2 * CHUNK, 2 * HEAD_DIM), BF16),
            pltpu.VMEM((2, n_chunks, CHUNK, HEAD_DIM), F32),
            dir_bf16(), dir_bf16(), dir_bf16(),
            pltpu.VMEM((S, HEAD_DIM), BF16), pltpu.VMEM((S, HEAD_DIM), BF16),
            pltpu.VMEM((S, 2 * HEAD_DIM), BF16),
            pltpu.VMEM((S, CHUNK), BF16),
            pltpu.VMEM((S, 2 * HEAD_DIM), BF16),
            pltpu.VMEM((n_chunks, HEAD_DIM, 2 * HEAD_DIM), F32),
            pltpu.VMEM((n_chunks, 8, 2 * HEAD_DIM), F32),
            pltpu.VMEM((n_chunks, HEAD_DIM, 2 * HEAD_DIM), BF16),
        ],
        compiler_params=pltpu.CompilerParams(
            dimension_semantics=("arbitrary", "arbitrary"),
            vmem_limit_bytes=V7X_VMEM_LIMIT),
        name="hgrn_pool",
    )(u3, w_heads, lb_rows, norm_g, tri, dmask)


def _merge_kernel(x_ref, u_ref, ya_ref, yp_ref, wg_ref, pw_ref, ps_ref, wa_ref, wb_ref, wo_ref, h_ref):
    D = x_ref.shape[1]
    n_grp = pw_ref.shape[0]
    gw = D // n_grp
    gates = jnp.dot(u_ref[...], wg_ref[...], preferred_element_type=F32)
    yb = jnp.concatenate(
        [jnp.dot(yp_ref[:, g * gw:(g + 1) * gw], pw_ref[g], preferred_element_type=F32)
         for g in range(n_grp)], axis=1)
    yb = (yb * ps_ref[...]).astype(BF16)
    za = jnp.dot(ya_ref[...], wa_ref[...], preferred_element_type=F32)
    zb = jnp.dot(yb, wb_ref[...], preferred_element_type=F32)
    merged = _sigmoid(gates[:, 0:D]) * za + _sigmoid(gates[:, D:2 * D]) * zb
    h_ref[...] = x_ref[...] + jnp.dot(merged.astype(BF16), wo_ref[...], preferred_element_type=F32)


def _const_spec(shape):
    return pl.BlockSpec(shape, lambda i: (0,) * len(shape), pipeline_mode=pl.Buffered(1))


def _merge(x2, u2, ya2, yp2, w_gates, pool_w, pool_scale, w_a, w_b, w_o, tile):
    T, D = x2.shape
    tok = lambda: pl.BlockSpec((tile, D), lambda i: (i, 0))
    return pl.pallas_call(
        _merge_kernel,
        grid=(T // tile,),
        in_specs=[tok(), tok(), tok(), tok(),
                  _const_spec(w_gates.shape), _const_spec(pool_w.shape), _const_spec((1, D)),
                  _const_spec(w_a.shape), _const_spec(w_b.shape), _const_spec(w_o.shape)],
        out_specs=tok(),
        out_shape=jax.ShapeDtypeStruct((T, D), F32),
        compiler_params=pltpu.CompilerParams(
            dimension_semantics=("arbitrary",), vmem_limit_bytes=V7X_VMEM_LIMIT),
        name="merge",
    )(x2, u2, ya2, yp2, w_gates, pool_w, pool_scale.reshape(1, D), w_a, w_b, w_o)


def _ffn_kernel(h_ref, g_ref, wi_ref, wo_ref, gfin_ref, o_ref, *, n_split, final_norm):
    h = h_ref[...]
    ms = jnp.mean(h * h, axis=-1, keepdims=True)
    u = (h * lax.rsqrt(ms + RMS_EPS) * g_ref[...]).astype(BF16)
    d_ff = wo_ref.shape[0]
    cw = d_ff // n_split
    acc = h
    for c in range(n_split):
        gate = jnp.dot(u, wi_ref[:, c * cw:(c + 1) * cw], preferred_element_type=F32)
        up = jnp.dot(u, wi_ref[:, d_ff + c * cw:d_ff + (c + 1) * cw], preferred_element_type=F32)
        act = (gate * _sigmoid(gate) * up).astype(BF16)
        acc = acc + jnp.dot(act, wo_ref[c * cw:(c + 1) * cw, :], preferred_element_type=F32)
    if final_norm:
        ms = jnp.mean(acc * acc, axis=-1, keepdims=True)
        acc = acc * lax.rsqrt(ms + RMS_EPS) * gfin_ref[...]
    o_ref[...] = acc


def _ffn(h2, g_ffn, w_in, w_out, g_final, tile, final_norm):
    T, D = h2.shape
    tok = lambda: pl.BlockSpec((tile, D), lambda i: (i, 0))
    return pl.pallas_call(
        functools.partial(_ffn_kernel, n_split=2, final_norm=final_norm),
        grid=(T // tile,),
        in_specs=[tok(), _const_spec((1, D)), _const_spec(w_in.shape), _const_spec(w_out.shape),
                  _const_spec((1, D))],
        out_specs=tok(),
        out_shape=jax.ShapeDtypeStruct((T, D), F32),
        compiler_params=pltpu.CompilerParams(
            dimension_semantics=("arbitrary",), vmem_limit_bytes=V7X_VMEM_LIMIT),
        name="ffn",
    )(h2, g_ffn.reshape(1, D), w_in, w_out, g_final.reshape(1, D))


def kernel(x, g_mix, w_in, lb_logits, hgrn_norm_g, pool_w, pool_scale, w_branch_a, w_branch_b,
           w_out, g_ffn, w_ffn_in, w_ffn_out, g_final):
    B, S, D = x.shape
    depth = w_in.shape[0]
    T = B * S
    assert D == HEADS * HEAD_DIM and S % 512 == 0 and max(POOL_WINDOWS) // 2 <= HALO
    n_head_cols = 6 * D
    tri_np, dmask_np = _chunk_constants()
    tri = jnp.asarray(tri_np, BF16)
    dmask = jnp.asarray(dmask_np, jnp.int32)
    lb_rows = lb_logits.astype(F32).reshape(2 * (depth + 1), D)

    h2 = x.reshape(T, D)
    for l in range(depth):
        w_l = w_in[l].astype(BF16)
        w_heads = (w_l[:, :n_head_cols].reshape(D, 6, HEADS, HEAD_DIM)
                   .transpose(2, 0, 1, 3).reshape(HEADS, D, 6 * HEAD_DIM))
        w_gates = w_l[:, n_head_cols:]
        u2 = _rmsnorm(h2, g_mix[l], tile=1024)
        ya, yp = _hgrn_pool(u2.reshape(B, S, D), w_heads, lb_rows, hgrn_norm_g[l].reshape(1, D),
                            tri, dmask, layer=l)
        h2 = _merge(h2, u2, ya.reshape(T, D), yp.reshape(T, D), w_gates,
                    pool_w[l].astype(BF16), pool_scale[l], w_branch_a[l].astype(BF16),
                    w_branch_b[l].astype(BF16), w_out[l].astype(BF16), tile=512)
        h2 = _ffn(h2, g_ffn[l], w_ffn_in[l].astype(BF16), w_ffn_out[l].astype(BF16), g_final,
                  tile=512, final_norm=(l == depth - 1))
    return h2.reshape(B, S, D)
```

```python
import functools
import math

import numpy as np
import jax
import jax.numpy as jnp
from jax import lax
from jax.experimental import pallas as pl
from jax.experimental.pallas import tpu as pltpu

F32 = jnp.float32
BF16 = jnp.bfloat16

RMS_EPS = 1e-6
HEADS = 8
HEAD_DIM = 128
POOL_WINDOWS = (2, 4, 8, 16)
CHUNK = 128
BLOCK = 32
HALO = 16
LOG2_E = 1.0 / math.log(2.0)
V7X_VMEM_LIMIT = 56 * 1024 * 1024

_NT = (((1,), (1,)), ((), ()))
_TN = (((0,), (0,)), ((), ()))


def _sigmoid(x):
    return 1.0 / (1.0 + jnp.exp(-x))


def _chunk_constants():
    t = np.arange(CHUNK)[:, None]
    s = np.arange(CHUNK)[None, :]
    tri = np.zeros((2 * CHUNK, 2 * CHUNK), np.float32)
    tri[:CHUNK, :CHUNK] = s <= t
    tri[CHUNK:, CHUNK:] = s >= t
    same = (t // BLOCK) == (s // BLOCK)
    cross = (t >= CHUNK // 2) != (s >= CHUNK // 2)
    dmask = np.stack([same & (s <= t), same & (s >= t), cross]).astype(np.int32)
    return tri, dmask


def _aligned(index, multiple):
    return index if isinstance(index, int) else pl.multiple_of(index, multiple)


def _split_hi_lo(a):
    hi = a.astype(BF16)
    lo = (a - hi.astype(F32)).astype(BF16)
    return jnp.concatenate([hi, lo], axis=1)


def _rmsnorm_kernel(x_ref, g_ref, o_ref):
    x = x_ref[...]
    ms = jnp.mean(x * x, axis=-1, keepdims=True)
    o_ref[...] = (x * lax.rsqrt(ms + RMS_EPS) * g_ref[...]).astype(o_ref.dtype)


def _rmsnorm(x2, g, tile):
    T, D = x2.shape
    return pl.pallas_call(
        _rmsnorm_kernel,
        grid=(T // tile,),
        in_specs=[pl.BlockSpec((tile, D), lambda i: (i, 0)),
                  pl.BlockSpec((1, D), lambda i: (0, 0))],
        out_specs=pl.BlockSpec((tile, D), lambda i: (i, 0)),
        out_shape=jax.ShapeDtypeStruct((T, D), BF16),
        compiler_params=pltpu.CompilerParams(dimension_semantics=("arbitrary",)),
        name="rmsnorm",
    )(x2, g.reshape(1, D))


def _hgrn_kernel(u_ref, w_ref, lbl_ref, ng_ref, tri_ref, dmask_ref, ya_ref, yp_ref,
                 raw_s, qs, kf, kb, g_s, vb, ogs, pbuf, pre_s, qc_s, kc_s, x32_s, x64q_s, x64k_s, kend_s,
                 a_s, qd, kv, dec, st,
                 *, seq, n_lb_rows, layer):
    C = CHUNK
    n_chunks = seq // C
    h = pl.program_id(1)

    def lower_bound(d):
        rows = [lbl_ref[d * n_lb_rows + r:d * n_lb_rows + r + 1, :] for r in range(n_lb_rows)]
        m = functools.reduce(jnp.maximum, rows)
        e = [jnp.exp(r - m) for r in rows]
        return sum(e[:layer + 1]) / sum(e)

    lb_f = lower_bound(0)
    lb_b = lower_bound(1)

    PR = 512
    norm_g = ng_ref[...]

    def proj_body(r, carry):
        rows = pl.ds(_aligned(r * PR, PR), PR)
        raw_s[rows, :] = jnp.dot(u_ref[rows, :], w_ref[...], preferred_element_type=F32)
        return carry

    def activation_block(r):
        rows = pl.ds(_aligned(r * PR, PR), PR)
        pr = raw_s[rows, :]
        q = pr[:, 0:128]
        qs[rows, :] = q * _sigmoid(q)
        f_f = lb_f + (1.0 - lb_f) * _sigmoid(pr[:, 128:256])
        g_f = _split_hi_lo(jnp.log(f_f) * LOG2_E)
        kf[rows, :] = 1.0 - f_f
        f_b = lb_b + (1.0 - lb_b) * _sigmoid(pr[:, 256:384])
        g_b = _split_hi_lo(jnp.log(f_b) * LOG2_E)
        for c in range(PR // C):
            g_s[r * (PR // C) + c, 0:C, :] = g_f[c * C:(c + 1) * C]
            g_s[r * (PR // C) + c, C:2 * C, :] = g_b[c * C:(c + 1) * C]
        kb[rows, :] = 1.0 - f_b
        vb[rows, :] = pr[:, 384:512].astype(BF16)
        og = pr[:, 512:640]
        ogs[rows, :] = og * _sigmoid(og) * norm_g
        pbuf[pl.ds(_aligned(r * PR + C, C), PR), :] = _split_hi_lo(pr[:, 640:768])

    pbuf[0:C, :] = jnp.zeros((C, 2 * HEAD_DIM), BF16)
    pbuf[seq + C:seq + 2 * C, :] = jnp.zeros((C, 2 * HEAD_DIM), BF16)

    n_blk = C // BLOCK
    zero_blk = jnp.zeros((BLOCK, HEAD_DIM), BF16)

    def place(blocks):
        return jnp.concatenate([blocks.get(b, zero_blk) for b in range(n_blk)], axis=0)

    def prefix_body(j, carry):
        p2 = jnp.dot(tri_ref[...], g_s[j], preferred_element_type=F32)
        pre = p2[:, 0:128] + p2[:, 128:256]
        pre_s[0, j] = pre[0:C]
        pre_s[1, j] = pre[C:2 * C]
        return carry

    def operand_body(j, carry):
        rows = pl.ds(_aligned(j * C, C), C)
        q = qs[rows, :]
        qd_parts, kend_parts, dec_parts = [], [], []
        q64, k64 = {}, {}
        for d, k_ref in enumerate((kf, kb)):
            fwd = d == 0
            k = k_ref[rows, :]
            pre = pre_s[d, j]

            def anchor(r, d=d):
                return pre_s[d, j, r:r + 1, :]

            a_64 = anchor(C // 2 - 1 if fwd else C // 2)
            a_end = anchor(C - 1 if fwd else 0)
            qc, kc, x32, qdb, keb = [], [], [], [], []
            for b in range(n_blk):
                lo = b * BLOCK
                pre_b, q_b, k_b = pre[lo:lo + BLOCK], q[lo:lo + BLOCK], k[lo:lo + BLOCK]
                mid = anchor(lo + BLOCK // 2 - 1 if fwd else lo + BLOCK // 2)
                e_mid = pre_b - mid
                qc_b = (q_b * jnp.exp2(e_mid)).astype(BF16)
                kc_b = (k_b * jnp.exp2(-e_mid)).astype(BF16)
                qc.append(qc_b)
                kc.append(kc_b)

                def factor(e):
                    return jnp.broadcast_to(jnp.exp2(e), (BLOCK, HEAD_DIM)).astype(BF16)

                if (b % 2 == 1) == fwd:
                    x32.append(qc_b * factor(mid - anchor(lo - 1 if fwd else lo + BLOCK)))
                else:
                    x32.append(kc_b * factor(anchor(lo + BLOCK - 1 if fwd else lo) - mid))
                if (b >= n_blk // 2) == fwd:
                    q64[b] = qc_b * factor(mid - a_64)
                else:
                    k64[b] = kc_b * factor(a_64 - mid)
                qdb.append(qc_b * factor(mid))
                keb.append(kc_b * factor(a_end - mid))
            qc_s[d, rows, :] = jnp.concatenate(qc, axis=0)
            kc_s[d, rows, :] = jnp.concatenate(kc, axis=0)
            x32_s[d, rows, :] = jnp.concatenate(x32, axis=0)
            qd_parts.append(jnp.concatenate(qdb, axis=0))
            kend_parts.append(jnp.concatenate(keb, axis=0))
            dec_parts.append(jnp.exp2(a_end))
        x64q_s[rows, :] = jnp.concatenate([q64[b] for b in range(n_blk)], axis=0)
        x64k_s[rows, :] = jnp.concatenate([k64[b] for b in range(n_blk)], axis=0)
        qd[rows, :] = jnp.concatenate(qd_parts, axis=1)
        kend_s[rows, :] = jnp.concatenate(kend_parts, axis=1)
        dec[j] = jnp.broadcast_to(jnp.concatenate(dec_parts, axis=1), (8, 2 * HEAD_DIM))
        return carry

    def chunk_body(j, carry):
        rows = pl.ds(_aligned(j * C, C), C)
        lhs32, rhs32 = [], []
        a_near = None
        for d in range(2):
            p_d = lax.dot_general(qc_s[d, rows, :], kc_s[d, rows, :], _NT, preferred_element_type=F32)
            a_d = jnp.where(dmask_ref[d] != 0, p_d, 0.0)
            a_near = a_d if a_near is None else a_near + a_d
            x32 = x32_s[d, rows, :]
            blocks = {b: x32[b * BLOCK:(b + 1) * BLOCK] for b in range(n_blk)}
            q_blocks = [b for b in range(n_blk) if (b % 2 == 1) == (d == 0)]
            lhs32.append(place({b: x for b, x in blocks.items() if b in q_blocks}))
            rhs32.append(place({b: x for b, x in blocks.items() if b not in q_blocks}))
        p_32 = lax.dot_general(jnp.concatenate(lhs32, axis=1), jnp.concatenate(rhs32, axis=1),
                               _NT, preferred_element_type=F32)
        p_64 = lax.dot_general(x64q_s[rows, :], x64k_s[rows, :], _NT, preferred_element_type=F32)
        a_s[rows, :] = jnp.where(dmask_ref[2] != 0, p_64, a_near + p_32).astype(BF16)
        kv[j] = lax.dot_general(vb[rows, :], kend_s[rows, :], _TN, preferred_element_type=F32)
        return carry

    chunks_per_block = PR // C
    n_blocks = seq // PR

    def local_block(r):
        activation_block(r)
        for c in range(chunks_per_block):
            prefix_body(r * chunks_per_block + c, 0)
        for c in range(chunks_per_block):
            j = r * chunks_per_block + c
            chunk_body(j, operand_body(j, 0))

    proj_body(0, 0)
    for r in range(1, n_blocks):
        proj_body(r, 0)
        local_block(r - 1)
    local_block(n_blocks - 1)

    def fwd_body(j, s):
        st[j, :, 0:HEAD_DIM] = s.astype(BF16)
        return s * dec[j, 0:1, 0:HEAD_DIM] + kv[j, :, 0:HEAD_DIM]

    lax.fori_loop(0, n_chunks, fwd_body, jnp.zeros((HEAD_DIM, HEAD_DIM), F32))

    def bwd_body(i, s):
        j = n_chunks - 1 - i
        st[j, :, HEAD_DIM:2 * HEAD_DIM] = s.astype(BF16)
        return s * dec[j, 0:1, HEAD_DIM:2 * HEAD_DIM] + kv[j, :, HEAD_DIM:2 * HEAD_DIM]

    lax.fori_loop(0, n_chunks, bwd_body, jnp.zeros((HEAD_DIM, HEAD_DIM), F32))

    grp = h // (HEADS // len(POOL_WINDOWS))
    half = sum(jnp.where(grp == i, w // 2, 0) for i, w in enumerate(POOL_WINDOWS))
    t_loc = lax.broadcasted_iota(jnp.int32, (C, C + 2 * HALO), 0)
    s_ext = lax.broadcasted_iota(jnp.int32, (C, C + 2 * HALO), 1)
    off = s_ext - HALO - t_loc
    band = jnp.where((off >= 1 - half) & (off <= half), 1.0, 0.0).astype(BF16)
    inv_w = sum(jnp.where(grp == i, 1.0 / w, 0.0) for i, w in enumerate(POOL_WINDOWS))

    def out_chunk(j, clipped):
        start = _aligned(j * C, C)
        rows = pl.ds(start, C)
        o = (jnp.dot(a_s[rows, :], vb[rows, :], preferred_element_type=F32)
             + lax.dot_general(qd[rows, :], st[j], _NT, preferred_element_type=F32))
        ms = jnp.mean(o * o, axis=-1, keepdims=True)
        ya_ref[rows, :] = (o * lax.rsqrt(ms + RMS_EPS) * ogs[rows, :]).astype(ya_ref.dtype)
        ws2 = jnp.dot(band, pbuf[pl.ds(_aligned(start + C - HALO, HALO), C + 2 * HALO), :],
                      preferred_element_type=F32)
        ws = ws2[:, 0:128] + ws2[:, 128:256]
        if clipped:
            t = j * C + lax.broadcasted_iota(jnp.int32, (C, HEAD_DIM), 0)
            cnt = jnp.minimum(t + half + 1, seq) - jnp.maximum(t - half + 1, 0)
            mean = ws / cnt.astype(F32)
        else:
            mean = ws * inv_w
        yp_ref[rows, :] = (mean - raw_s[rows, 5 * HEAD_DIM:6 * HEAD_DIM]).astype(yp_ref.dtype)

    edge_chunks = sorted({0, n_chunks - 1})
    for j in edge_chunks:
        out_chunk(j, clipped=True)

    def out_body(j, carry):
        out_chunk(j, clipped=False)
        return carry

    lax.fori_loop(1, n_chunks - 1, out_body, 0, unroll=7)


def _hgrn_pool(u3, w_heads, lb_rows, norm_g, tri, dmask, layer):
    B, S, D = u3.shape
    n_lb_rows = lb_rows.shape[0] // 2
    n_chunks = S // CHUNK
    row_f32 = lambda: pltpu.VMEM((S, HEAD_DIM), F32)
    dir_bf16 = lambda: pltpu.VMEM((2, S, HEAD_DIM), BF16)
    out_spec = pl.BlockSpec((None, S, HEAD_DIM), lambda b, h: (b, 0, h))
    return pl.pallas_call(
        functools.partial(_hgrn_kernel, seq=S, n_lb_rows=n_lb_rows, layer=layer),
        grid=(B, HEADS),
        in_specs=[
            pl.BlockSpec((None, S, D), lambda b, h: (b, 0, 0)),
            pl.BlockSpec((None, D, 6 * HEAD_DIM), lambda b, h: (h, 0, 0)),
            pl.BlockSpec((2 * n_lb_rows, HEAD_DIM), lambda b, h: (0, h)),
            pl.BlockSpec((1, HEAD_DIM), lambda b, h: (0, h)),
            pl.BlockSpec((2 * CHUNK, 2 * CHUNK), lambda b, h: (0, 0)),
            pl.BlockSpec((3, CHUNK, CHUNK), lambda b, h: (0, 0, 0)),
        ],
        out_specs=[out_spec, out_spec],
        out_shape=[jax.ShapeDtypeStruct((B, S, D), BF16), jax.ShapeDtypeStruct((B, S, D), BF16)],
        scratch_shapes=[
            pltpu.VMEM((S, 6 * HEAD_DIM), F32),
            row_f32(), row_f32(), row_f32(),
            pltpu.VMEM((n_chunks, 2 * CHUNK, 2 * HEAD_DIM), BF16),
            pltpu.VMEM((S, HEAD_DIM), BF16),
            row_f32(),
            pltpu.VMEM((S + 2 * CHUNK, 2 * HEAD_DIM), BF16),
            pltpu.VMEM((2, n_chunks, CHUNK, HEAD_DIM), F32),
            dir_bf16(), dir_bf16(), dir_bf16(),
            pltpu.VMEM((S, HEAD_DIM), BF16), pltpu.VMEM((S, HEAD_DIM), BF16),
            pltpu.VMEM((S, 2 * HEAD_DIM), BF16),
            pltpu.VMEM((S, CHUNK), BF16),
            pltpu.VMEM((S, 2 * HEAD_DIM), BF16),
            pltpu.VMEM((n_chunks, HEAD_DIM, 2 * HEAD_DIM), F32),
            pltpu.VMEM((n_chunks, 8, 2 * HEAD_DIM), F32),
            pltpu.VMEM((n_chunks, HEAD_DIM, 2 * HEAD_DIM), BF16),
        ],
        compiler_params=pltpu.CompilerParams(
            dimension_semantics=("arbitrary", "arbitrary"),
            vmem_limit_bytes=V7X_VMEM_LIMIT),
        name="hgrn_pool",
    )(u3, w_heads, lb_rows, norm_g, tri, dmask)


def _merge_kernel(x_ref, u_ref, ya_ref, yp_ref, wg_ref, pw_ref, ps_ref, wa_ref, wb_ref, wo_ref, h_ref):
    D = x_ref.shape[1]
    n_grp = pw_ref.shape[0]
    gw = D // n_grp
    gates = jnp.dot(u_ref[...], wg_ref[...], preferred_element_type=F32)
    yb = jnp.concatenate(
        [jnp.dot(yp_ref[:, g * gw:(g + 1) * gw], pw_ref[g], preferred_element_type=F32)
         for g in range(n_grp)], axis=1)
    yb = (yb * ps_ref[...]).astype(BF16)
    za = jnp.dot(ya_ref[...], wa_ref[...], preferred_element_type=F32)
    zb = jnp.dot(yb, wb_ref[...], preferred_element_type=F32)
    merged = _sigmoid(gates[:, 0:D]) * za + _sigmoid(gates[:, D:2 * D]) * zb
    h_ref[...] = x_ref[...] + jnp.dot(merged.astype(BF16), wo_ref[...], preferred_element_type=F32)


def _const_spec(shape):
    return pl.BlockSpec(shape, lambda i: (0,) * len(shape), pipeline_mode=pl.Buffered(1))


def _merge(x2, u2, ya2, yp2, w_gates, pool_w, pool_scale, w_a, w_b, w_o, tile):
    T, D = x2.shape
    tok = lambda: pl.BlockSpec((tile, D), lambda i: (i, 0))
    return pl.pallas_call(
        _merge_kernel,
        grid=(T // tile,),
        in_specs=[tok(), tok(), tok(), tok(),
                  _const_spec(w_gates.shape), _const_spec(pool_w.shape), _const_spec((1, D)),
                  _const_spec(w_a.shape), _const_spec(w_b.shape), _const_spec(w_o.shape)],
        out_specs=tok(),
        out_shape=jax.ShapeDtypeStruct((T, D), F32),
        compiler_params=pltpu.CompilerParams(
            dimension_semantics=("arbitrary",), vmem_limit_bytes=V7X_VMEM_LIMIT),
        name="merge",
    )(x2, u2, ya2, yp2, w_gates, pool_w, pool_scale.reshape(1, D), w_a, w_b, w_o)


def _ffn_kernel(h_ref, g_ref, wi_ref, wo_ref, gfin_ref, o_ref, *, n_split, final_norm):
    h = h_ref[...]
    ms = jnp.mean(h * h, axis=-1, keepdims=True)
    u = (h * lax.rsqrt(ms + RMS_EPS) * g_ref[...]).astype(BF16)
    d_ff = wo_ref.shape[0]
    cw = d_ff // n_split
    acc = h
    for c in range(n_split):
        gate = jnp.dot(u, wi_ref[:, c * cw:(c + 1) * cw], preferred_element_type=F32)
        up = jnp.dot(u, wi_ref[:, d_ff + c * cw:d_ff + (c + 1) * cw], preferred_element_type=F32)
        act = (gate * _sigmoid(gate) * up).astype(BF16)
        acc = acc + jnp.dot(act, wo_ref[c * cw:(c + 1) * cw, :], preferred_element_type=F32)
    if final_norm:
        ms = jnp.mean(acc * acc, axis=-1, keepdims=True)
        acc = acc * lax.rsqrt(ms + RMS_EPS) * gfin_ref[...]
    o_ref[...] = acc


def _ffn(h2, g_ffn, w_in, w_out, g_final, tile, final_norm):
    T, D = h2.shape
    tok = lambda: pl.BlockSpec((tile, D), lambda i: (i, 0))
    return pl.pallas_call(
        functools.partial(_ffn_kernel, n_split=2, final_norm=final_norm),
        grid=(T // tile,),
        in_specs=[tok(), _const_spec((1, D)), _const_spec(w_in.shape), _const_spec(w_out.shape),
                  _const_spec((1, D))],
        out_specs=tok(),
        out_shape=jax.ShapeDtypeStruct((T, D), F32),
        compiler_params=pltpu.CompilerParams(
            dimension_semantics=("arbitrary",), vmem_limit_bytes=V7X_VMEM_LIMIT),
        name="ffn",
    )(h2, g_ffn.reshape(1, D), w_in, w_out, g_final.reshape(1, D))


def kernel(x, g_mix, w_in, lb_logits, hgrn_norm_g, pool_w, pool_scale, w_branch_a, w_branch_b,
           w_out, g_ffn, w_ffn_in, w_ffn_out, g_final):
    B, S, D = x.shape
    depth = w_in.shape[0]
    T = B * S
    assert D == HEADS * HEAD_DIM and S % 512 == 0 and max(POOL_WINDOWS) // 2 <= HALO
    n_head_cols = 6 * D
    tri_np, dmask_np = _chunk_constants()
    tri = jnp.asarray(tri_np, BF16)
    dmask = jnp.asarray(dmask_np, jnp.int32)
    lb_rows = lb_logits.astype(F32).reshape(2 * (depth + 1), D)

    h2 = x.reshape(T, D)
    for l in range(depth):
        w_l = w_in[l].astype(BF16)
        w_heads = (w_l[:, :n_head_cols].reshape(D, 6, HEADS, HEAD_DIM)
                   .transpose(2, 0, 1, 3).reshape(HEADS, D, 6 * HEAD_DIM))
        w_gates = w_l[:, n_head_cols:]
        u2 = _rmsnorm(h2, g_mix[l], tile=1024)
        ya, yp = _hgrn_pool(u2.reshape(B, S, D), w_heads, lb_rows, hgrn_norm_g[l].reshape(1, D),
                            tri, dmask, layer=l)
        h2 = _merge(h2, u2, ya.reshape(T, D), yp.reshape(T, D), w_gates,
                    pool_w[l].astype(BF16), pool_scale[l], w_branch_a[l].astype(BF16),
                    w_branch_b[l].astype(BF16), w_out[l].astype(BF16), tile=512)
        h2 = _ffn(h2, g_ffn[l], w_ffn_in[l].astype(BF16), w_ffn_out[l].astype(BF16), g_final,
                  tile=512, final_norm=(l == depth - 1))
    return h2.reshape(B, S, D)
```

```python
import functools
import math

import numpy as np
import jax
import jax.numpy as jnp
from jax import lax
from jax.experimental import pallas as pl
from jax.experimental.pallas import tpu as pltpu

F32 = jnp.float32
BF16 = jnp.bfloat16

RMS_EPS = 1e-6
HEADS = 8
HEAD_DIM = 128
POOL_WINDOWS = (2, 4, 8, 16)
CHUNK = 128
BLOCK = 32
HALO = 16
LOG2_E = 1.0 / math.log(2.0)
V7X_VMEM_LIMIT = 56 * 1024 * 1024

_NT = (((1,), (1,)), ((), ()))
_TN = (((0,), (0,)), ((), ()))


def _sigmoid(x):
    return 1.0 / (1.0 + jnp.exp(-x))


def _chunk_constants():
    t = np.arange(CHUNK)[:, None]
    s = np.arange(CHUNK)[None, :]
    tri = np.zeros((2 * CHUNK, 2 * CHUNK), np.float32)
    tri[:CHUNK, :CHUNK] = s <= t
    tri[CHUNK:, CHUNK:] = s >= t
    same = (t // BLOCK) == (s // BLOCK)
    cross = (t >= CHUNK // 2) != (s >= CHUNK // 2)
    dmask = np.stack([same & (s <= t), same & (s >= t), cross]).astype(np.int32)
    return tri, dmask


def _aligned(index, multiple):
    return index if isinstance(index, int) else pl.multiple_of(index, multiple)


def _split_hi_lo(a):
    hi = a.astype(BF16)
    lo = (a - hi.astype(F32)).astype(BF16)
    return jnp.concatenate([hi, lo], axis=1)


def _rmsnorm_kernel(x_ref, g_ref, o_ref):
    x = x_ref[...]
    ms = jnp.mean(x * x, axis=-1, keepdims=True)
    o_ref[...] = (x * lax.rsqrt(ms + RMS_EPS) * g_ref[...]).astype(o_ref.dtype)


def _rmsnorm(x2, g, tile):
    T, D = x2.shape
    return pl.pallas_call(
        _rmsnorm_kernel,
        grid=(T // tile,),
        in_specs=[pl.BlockSpec((tile, D), lambda i: (i, 0)),
                  pl.BlockSpec((1, D), lambda i: (0, 0))],
        out_specs=pl.BlockSpec((tile, D), lambda i: (i, 0)),
        out_shape=jax.ShapeDtypeStruct((T, D), BF16),
        compiler_params=pltpu.CompilerParams(dimension_semantics=("arbitrary",)),
        name="rmsnorm",
    )(x2, g.reshape(1, D))


def _hgrn_kernel(u_ref, wq_ref, wff_ref, wfb_ref, wi_ref, wog_ref, wp_ref, lbl_ref, ng_ref, tri_ref, dmask_ref,
                 ya_ref, yp_ref, w_s, raw_s, qs, kf, kb, g_s, vb, ogs, pbuf, pre_s, qc_s, kc_s, x32_s, x64q_s, x64k_s, kend_s,
                 a_s, qd, kv, dec, st,
                 *, seq, n_lb_rows, layer):
    C = CHUNK
    n_chunks = seq // C
    h = pl.program_id(1)

    def lower_bound(d):
        rows = [lbl_ref[d * n_lb_rows + r:d * n_lb_rows + r + 1, :] for r in range(n_lb_rows)]
        m = functools.reduce(jnp.maximum, rows)
        e = [jnp.exp(r - m) for r in rows]
        return sum(e[:layer + 1]) / sum(e)

    lb_f = lower_bound(0)
    lb_b = lower_bound(1)

    for i, w_ref in enumerate((wq_ref, wff_ref, wfb_ref, wi_ref, wog_ref, wp_ref)):
        w_s[:, i * HEAD_DIM:(i + 1) * HEAD_DIM] = w_ref[...]
    PR = 512
    norm_g = ng_ref[...]

    def proj_body(r, carry):
        rows = pl.ds(_aligned(r * PR, PR), PR)
        raw_s[rows, :] = jnp.dot(u_ref[rows, :], w_s[...], preferred_element_type=F32)
        return carry

    def activation_block(r):
        rows = pl.ds(_aligned(r * PR, PR), PR)
        pr = raw_s[rows, :]
        q = pr[:, 0:128]
        qs[rows, :] = q * _sigmoid(q)
        f_f = lb_f + (1.0 - lb_f) * _sigmoid(pr[:, 128:256])
        g_f = _split_hi_lo(jnp.log(f_f) * LOG2_E)
        kf[rows, :] = 1.0 - f_f
        f_b = lb_b + (1.0 - lb_b) * _sigmoid(pr[:, 256:384])
        g_b = _split_hi_lo(jnp.log(f_b) * LOG2_E)
        for c in range(PR // C):
            g_s[r * (PR // C) + c, 0:C, :] = g_f[c * C:(c + 1) * C]
            g_s[r * (PR // C) + c, C:2 * C, :] = g_b[c * C:(c + 1) * C]
        kb[rows, :] = 1.0 - f_b
        vb[rows, :] = pr[:, 384:512].astype(BF16)
        og = pr[:, 512:640]
        ogs[rows, :] = og * _sigmoid(og) * norm_g
        pbuf[pl.ds(_aligned(r * PR + C, C), PR), :] = _split_hi_lo(pr[:, 640:768])

    pbuf[0:C, :] = jnp.zeros((C, 2 * HEAD_DIM), BF16)
    pbuf[seq + C:seq + 2 * C, :] = jnp.zeros((C, 2 * HEAD_DIM), BF16)

    n_blk = C // BLOCK
    zero_blk = jnp.zeros((BLOCK, HEAD_DIM), BF16)

    def place(blocks):
        return jnp.concatenate([blocks.get(b, zero_blk) for b in range(n_blk)], axis=0)

    def prefix_body(j, carry):
        p2 = jnp.dot(tri_ref[...], g_s[j], preferred_element_type=F32)
        pre = p2[:, 0:128] + p2[:, 128:256]
        pre_s[0, j] = pre[0:C]
        pre_s[1, j] = pre[C:2 * C]
        return carry

    def operand_body(j, carry):
        rows = pl.ds(_aligned(j * C, C), C)
        q = qs[rows, :]
        qd_parts, kend_parts, dec_parts = [], [], []
        q64, k64 = {}, {}
        for d, k_ref in enumerate((kf, kb)):
            fwd = d == 0
            k = k_ref[rows, :]
            pre = pre_s[d, j]

            def anchor(r, d=d):
                return pre_s[d, j, r:r + 1, :]

            a_64 = anchor(C // 2 - 1 if fwd else C // 2)
            a_end = anchor(C - 1 if fwd else 0)
            qc, kc, x32, qdb, keb = [], [], [], [], []
            for b in range(n_blk):
                lo = b * BLOCK
                pre_b, q_b, k_b = pre[lo:lo + BLOCK], q[lo:lo + BLOCK], k[lo:lo + BLOCK]
                mid = anchor(lo + BLOCK // 2 - 1 if fwd else lo + BLOCK // 2)
                e_mid = pre_b - mid
                qc_b = (q_b * jnp.exp2(e_mid)).astype(BF16)
                kc_b = (k_b * jnp.exp2(-e_mid)).astype(BF16)
                qc.append(qc_b)
                kc.append(kc_b)

                def factor(e):
                    return jnp.broadcast_to(jnp.exp2(e), (BLOCK, HEAD_DIM)).astype(BF16)

                if (b % 2 == 1) == fwd:
                    x32.append(qc_b * factor(mid - anchor(lo - 1 if fwd else lo + BLOCK)))
                else:
                    x32.append(kc_b * factor(anchor(lo + BLOCK - 1 if fwd else lo) - mid))
                if (b >= n_blk // 2) == fwd:
                    q64[b] = qc_b * factor(mid - a_64)
                else:
                    k64[b] = kc_b * factor(a_64 - mid)
                qdb.append(qc_b * factor(mid))
                keb.append(kc_b * factor(a_end - mid))
            qc_s[d, rows, :] = jnp.concatenate(qc, axis=0)
            kc_s[d, rows, :] = jnp.concatenate(kc, axis=0)
            x32_s[d, rows, :] = jnp.concatenate(x32, axis=0)
            qd_parts.append(jnp.concatenate(qdb, axis=0))
            kend_parts.append(jnp.concatenate(keb, axis=0))
            dec_parts.append(jnp.exp2(a_end))
        x64q_s[rows, :] = jnp.concatenate([q64[b] for b in range(n_blk)], axis=0)
        x64k_s[rows, :] = jnp.concatenate([k64[b] for b in range(n_blk)], axis=0)
        qd[rows, :] = jnp.concatenate(qd_parts, axis=1)
        kend_s[rows, :] = jnp.concatenate(kend_parts, axis=1)
        dec[j] = jnp.broadcast_to(jnp.concatenate(dec_parts, axis=1), (8, 2 * HEAD_DIM))
        return carry

    def chunk_body(j, carry):
        rows = pl.ds(_aligned(j * C, C), C)
        lhs32, rhs32 = [], []
        zero_c = jnp.zeros((C, HEAD_DIM), BF16)
        k_diag = jnp.concatenate(
            [jnp.concatenate([kc_s[0, rows, :], zero_c], axis=1),
             jnp.concatenate([zero_c, kc_s[1, rows, :]], axis=1)], axis=0)
        p_c = lax.dot_general(jnp.concatenate([qc_s[0, rows, :], qc_s[1, rows, :]], axis=1), k_diag,
                              _NT, preferred_element_type=F32)
        a_near = (jnp.where(dmask_ref[0] != 0, p_c[:, 0:C], 0.0)
                  + jnp.where(dmask_ref[1] != 0, p_c[:, C:2 * C], 0.0))
        for d in range(2):
            x32 = x32_s[d, rows, :]
            blocks = {b: x32[b * BLOCK:(b + 1) * BLOCK] for b in range(n_blk)}
            q_blocks = [b for b in range(n_blk) if (b % 2 == 1) == (d == 0)]
            lhs32.append(place({b: x for b, x in blocks.items() if b in q_blocks}))
            rhs32.append(place({b: x for b, x in blocks.items() if b not in q_blocks}))
        p_32 = lax.dot_general(jnp.concatenate(lhs32, axis=1), jnp.concatenate(rhs32, axis=1),
                               _NT, preferred_element_type=F32)
        p_64 = lax.dot_general(x64q_s[rows, :], x64k_s[rows, :], _NT, preferred_element_type=F32)
        a_s[rows, :] = jnp.where(dmask_ref[2] != 0, p_64, a_near + p_32).astype(BF16)
        kv[j] = lax.dot_general(vb[rows, :], kend_s[rows, :], _TN, preferred_element_type=F32)
        return carry

    chunks_per_block = PR // C
    n_blocks = seq // PR

    def local_block(r):
        activation_block(r)
        for c in range(chunks_per_block):
            prefix_body(r * chunks_per_block + c, 0)
        for c in range(chunks_per_block):
            j = r * chunks_per_block + c
            chunk_body(j, operand_body(j, 0))

    proj_body(0, 0)
    for r in range(1, n_blocks):
        proj_body(r, 0)
        local_block(r - 1)
    local_block(n_blocks - 1)

    def fwd_body(j, s):
        st[j, :, 0:HEAD_DIM] = s.astype(BF16)
        return s * dec[j, 0:1, 0:HEAD_DIM] + kv[j, :, 0:HEAD_DIM]

    lax.fori_loop(0, n_chunks, fwd_body, jnp.zeros((HEAD_DIM, HEAD_DIM), F32))

    def bwd_body(i, s):
        j = n_chunks - 1 - i
        st[j, :, HEAD_DIM:2 * HEAD_DIM] = s.astype(BF16)
        return s * dec[j, 0:1, HEAD_DIM:2 * HEAD_DIM] + kv[j, :, HEAD_DIM:2 * HEAD_DIM]

    lax.fori_loop(0, n_chunks, bwd_body, jnp.zeros((HEAD_DIM, HEAD_DIM), F32))

    grp = h // (HEADS // len(POOL_WINDOWS))
    half = sum(jnp.where(grp == i, w // 2, 0) for i, w in enumerate(POOL_WINDOWS))
    t_loc = lax.broadcasted_iota(jnp.int32, (C, C + 2 * HALO), 0)
    s_ext = lax.broadcasted_iota(jnp.int32, (C, C + 2 * HALO), 1)
    off = s_ext - HALO - t_loc
    band = jnp.where((off >= 1 - half) & (off <= half), 1.0, 0.0).astype(BF16)
    inv_w = sum(jnp.where(grp == i, 1.0 / w, 0.0) for i, w in enumerate(POOL_WINDOWS))

    def out_chunk(j, clipped):
        start = _aligned(j * C, C)
        rows = pl.ds(start, C)
        o = (jnp.dot(a_s[rows, :], vb[rows, :], preferred_element_type=F32)
             + lax.dot_general(qd[rows, :], st[j], _NT, preferred_element_type=F32))
        ms = jnp.mean(o * o, axis=-1, keepdims=True)
        ya_ref[rows, :] = (o * lax.rsqrt(ms + RMS_EPS) * ogs[rows, :]).astype(ya_ref.dtype)
        ws2 = jnp.dot(band, pbuf[pl.ds(_aligned(start + C - HALO, HALO), C + 2 * HALO), :],
                      preferred_element_type=F32)
        ws = ws2[:, 0:128] + ws2[:, 128:256]
        if clipped:
            t = j * C + lax.broadcasted_iota(jnp.int32, (C, HEAD_DIM), 0)
            cnt = jnp.minimum(t + half + 1, seq) - jnp.maximum(t - half + 1, 0)
            mean = ws / cnt.astype(F32)
        else:
            mean = ws * inv_w
        yp_ref[rows, :] = (mean - raw_s[rows, 5 * HEAD_DIM:6 * HEAD_DIM]).astype(yp_ref.dtype)

    for j in range(n_chunks):
        out_chunk(j, clipped=j in (0, n_chunks - 1))


def _hgrn_pool(u3, w_l, lb_rows, norm_g, tri, dmask, layer):
    B, S, D = u3.shape
    n_lb_rows = lb_rows.shape[0] // 2
    n_chunks = S // CHUNK
    row_f32 = lambda: pltpu.VMEM((S, HEAD_DIM), F32)
    dir_bf16 = lambda: pltpu.VMEM((2, S, HEAD_DIM), BF16)
    out_spec = pl.BlockSpec((None, S, HEAD_DIM), lambda b, h: (b, 0, h))
    return pl.pallas_call(
        functools.partial(_hgrn_kernel, seq=S, n_lb_rows=n_lb_rows, layer=layer),
        grid=(B, HEADS),
        in_specs=[
            pl.BlockSpec((None, S, D), lambda b, h: (b, 0, 0)),
            *[pl.BlockSpec((D, HEAD_DIM), functools.partial(lambda b, h, i: (0, i * HEADS + h), i=i))
              for i in range(6)],
            pl.BlockSpec((2 * n_lb_rows, HEAD_DIM), lambda b, h: (0, h)),
            pl.BlockSpec((1, HEAD_DIM), lambda b, h: (0, h)),
            pl.BlockSpec((2 * CHUNK, 2 * CHUNK), lambda b, h: (0, 0)),
            pl.BlockSpec((3, CHUNK, CHUNK), lambda b, h: (0, 0, 0)),
        ],
        out_specs=[out_spec, out_spec],
        out_shape=[jax.ShapeDtypeStruct((B, S, D), BF16), jax.ShapeDtypeStruct((B, S, D), BF16)],
        scratch_shapes=[
            pltpu.VMEM((D, 6 * HEAD_DIM), BF16),
            pltpu.VMEM((S, 6 * HEAD_DIM), F32),
            row_f32(), row_f32(), row_f32(),
            pltpu.VMEM((n_chunks, 2 * CHUNK, 2 * HEAD_DIM), BF16),
            pltpu.VMEM((S, HEAD_DIM), BF16),
            row_f32(),
            pltpu.VMEM((S + 2 * CHUNK, 2 * HEAD_DIM), BF16),
            pltpu.VMEM((2, n_chunks, CHUNK, HEAD_DIM), F32),
            dir_bf16(), dir_bf16(), dir_bf16(),
            pltpu.VMEM((S, HEAD_DIM), BF16), pltpu.VMEM((S, HEAD_DIM), BF16),
            pltpu.VMEM((S, 2 * HEAD_DIM), BF16),
            pltpu.VMEM((S, CHUNK), BF16),
            pltpu.VMEM((S, 2 * HEAD_DIM), BF16),
            pltpu.VMEM((n_chunks, HEAD_DIM, 2 * HEAD_DIM), F32),
            pltpu.VMEM((n_chunks, 8, 2 * HEAD_DIM), F32),
            pltpu.VMEM((n_chunks, HEAD_DIM, 2 * HEAD_DIM), BF16),
        ],
        compiler_params=pltpu.CompilerParams(
            dimension_semantics=("arbitrary", "arbitrary"),
            vmem_limit_bytes=V7X_VMEM_LIMIT),
        name="hgrn_pool",
    )(u3, *([w_l] * 6), lb_rows, norm_g, tri, dmask)


def _merge_kernel(x_ref, u_ref, ya_ref, yp_ref, wg_ref, pw_ref, ps_ref, wa_ref, wb_ref, wo_ref, h_ref):
    D = x_ref.shape[1]
    n_grp = pw_ref.shape[0]
    gw = D // n_grp
    gates = jnp.dot(u_ref[...], wg_ref[...], preferred_element_type=F32)
    yb = jnp.concatenate(
        [jnp.dot(yp_ref[:, g * gw:(g + 1) * gw], pw_ref[g], preferred_element_type=F32)
         for g in range(n_grp)], axis=1)
    yb = (yb * ps_ref[...]).astype(BF16)
    za = jnp.dot(ya_ref[...], wa_ref[...], preferred_element_type=F32)
    zb = jnp.dot(yb, wb_ref[...], preferred_element_type=F32)
    merged = _sigmoid(gates[:, 0:D]) * za + _sigmoid(gates[:, D:2 * D]) * zb
    h_ref[...] = x_ref[...] + jnp.dot(merged.astype(BF16), wo_ref[...], preferred_element_type=F32)


def _const_spec(shape):
    return pl.BlockSpec(shape, lambda i: (0,) * len(shape), pipeline_mode=pl.Buffered(1))


def _merge(x2, u2, ya2, yp2, w_l, pool_w, pool_scale, w_a, w_b, w_o, tile):
    T, D = x2.shape
    tok = lambda: pl.BlockSpec((tile, D), lambda i: (i, 0))
    return pl.pallas_call(
        _merge_kernel,
        grid=(T // tile,),
        in_specs=[tok(), tok(), tok(), tok(),
                  pl.BlockSpec((D, 2 * D), lambda i: (0, w_l.shape[1] // (2 * D) - 1),
                               pipeline_mode=pl.Buffered(1)),
                  _const_spec(pool_w.shape), _const_spec((1, D)),
                  _const_spec(w_a.shape), _const_spec(w_b.shape), _const_spec(w_o.shape)],
        out_specs=tok(),
        out_shape=jax.ShapeDtypeStruct((T, D), F32),
        compiler_params=pltpu.CompilerParams(
            dimension_semantics=("arbitrary",), vmem_limit_bytes=V7X_VMEM_LIMIT),
        name="merge",
    )(x2, u2, ya2, yp2, w_l, pool_w, pool_scale.reshape(1, D), w_a, w_b, w_o)


def _ffn_kernel(h_ref, g_ref, wi_ref, wo_ref, gfin_ref, o_ref, *, n_split, final_norm):
    h = h_ref[...]
    ms = jnp.mean(h * h, axis=-1, keepdims=True)
    u = (h * lax.rsqrt(ms + RMS_EPS) * g_ref[...]).astype(BF16)
    d_ff = wo_ref.shape[0]
    cw = d_ff // n_split
    acc = h
    for c in range(n_split):
        gate = jnp.dot(u, wi_ref[:, c * cw:(c + 1) * cw], preferred_element_type=F32)
        up = jnp.dot(u, wi_ref[:, d_ff + c * cw:d_ff + (c + 1) * cw], preferred_element_type=F32)
        act = (gate * _sigmoid(gate) * up).astype(BF16)
        acc = acc + jnp.dot(act, wo_ref[c * cw:(c + 1) * cw, :], preferred_element_type=F32)
    if final_norm:
        ms = jnp.mean(acc * acc, axis=-1, keepdims=True)
        acc = acc * lax.rsqrt(ms + RMS_EPS) * gfin_ref[...]
    o_ref[...] = acc


def _ffn(h2, g_ffn, w_in, w_out, g_final, tile, final_norm):
    T, D = h2.shape
    tok = lambda: pl.BlockSpec((tile, D), lambda i: (i, 0))
    return pl.pallas_call(
        functools.partial(_ffn_kernel, n_split=2, final_norm=final_norm),
        grid=(T // tile,),
        in_specs=[tok(), _const_spec((1, D)), _const_spec(w_in.shape), _const_spec(w_out.shape),
                  _const_spec((1, D))],
        out_specs=tok(),
        out_shape=jax.ShapeDtypeStruct((T, D), F32),
        compiler_params=pltpu.CompilerParams(
            dimension_semantics=("arbitrary",), vmem_limit_bytes=V7X_VMEM_LIMIT),
        name="ffn",
    )(h2, g_ffn.reshape(1, D), w_in, w_out, g_final.reshape(1, D))


def kernel(x, g_mix, w_in, lb_logits, hgrn_norm_g, pool_w, pool_scale, w_branch_a, w_branch_b,
           w_out, g_ffn, w_ffn_in, w_ffn_out, g_final):
    B, S, D = x.shape
    depth = w_in.shape[0]
    T = B * S
    assert D == HEADS * HEAD_DIM and S % 512 == 0 and max(POOL_WINDOWS) // 2 <= HALO
    assert w_in.shape[2] == 8 * D
    tri_np, dmask_np = _chunk_constants()
    tri = jnp.asarray(tri_np, BF16)
    dmask = jnp.asarray(dmask_np, jnp.int32)
    lb_rows = lb_logits.astype(F32).reshape(2 * (depth + 1), D)

    h2 = x.reshape(T, D)
    for l in range(depth):
        w_l = w_in[l].astype(BF16)
        u2 = _rmsnorm(h2, g_mix[l], tile=1024)
        ya, yp = _hgrn_pool(u2.reshape(B, S, D), w_l, lb_rows, hgrn_norm_g[l].reshape(1, D),
                            tri, dmask, layer=l)
        h2 = _merge(h2, u2, ya.reshape(T, D), yp.reshape(T, D), w_l,
                    pool_w[l].astype(BF16), pool_scale[l], w_branch_a[l].astype(BF16),
                    w_branch_b[l].astype(BF16), w_out[l].astype(BF16), tile=512)
        h2 = _ffn(h2, g_ffn[l], w_ffn_in[l].astype(BF16), w_ffn_out[l].astype(BF16), g_final,
                  tile=512, final_norm=(l == depth - 1))
    return h2.reshape(B, S, D)
```

```python
import functools
import math

import numpy as np
import jax
import jax.numpy as jnp
from jax import lax
from jax.experimental import pallas as pl
from jax.experimental.pallas import tpu as pltpu

F32 = jnp.float32
BF16 = jnp.bfloat16

RMS_EPS = 1e-6
HEADS = 8
HEAD_DIM = 128
POOL_WINDOWS = (2, 4, 8, 16)
CHUNK = 128
BLOCK = 32
HALO = 16
NORM_ROWS = 128
PROJ_ROWS = 512
EDGE_ROWS = 256
LOG2_E = 1.0 / math.log(2.0)
V7X_VMEM_LIMIT = 56 * 1024 * 1024
MXU_WIDTH = 256

_NT = (((1,), (1,)), ((), ()))
_TN = (((0,), (0,)), ((), ()))


def _sigmoid(x):
    return 1.0 / (1.0 + jnp.exp(-x))


def _chunk_constants():
    t = np.arange(CHUNK)[:, None]
    s = np.arange(CHUNK)[None, :]
    tri = np.stack([np.tile(s <= t, (1, 2)), np.tile(s >= t, (1, 2))]).astype(np.float32)
    same = (t // BLOCK) == (s // BLOCK)
    cross = (t >= CHUNK // 2) != (s >= CHUNK // 2)
    dmask = np.stack([same & (s <= t), same & (s >= t), cross, same]).astype(np.int32)
    return tri, dmask


def _aligned(index, multiple):
    return index if isinstance(index, int) else pl.multiple_of(index, multiple)


def _split_hi_lo(a):
    hi = a.astype(BF16)
    lo = (a - hi.astype(F32)).astype(BF16)
    return jnp.concatenate([hi, lo], axis=1)


def _rmsnorm_bf16(x, g):
    ms = jnp.mean(x * x, axis=-1, keepdims=True)
    return (x * lax.rsqrt(ms + RMS_EPS) * g).astype(BF16)


def _hgrn_kernel(x_ref, gmix_ref, wq_ref, wff_ref, wfb_ref, wi_ref, wog_ref, wp_ref, lbl_ref, ng_ref, tri_ref,
                 dmask_ref, ya_ref, yp_ref, u_s, w_s, raw_s, qs, kf, kb, g_s, vb, ogs, pbuf, pre_s, qc_s, kc_s, xq_s, x32k_s, x64k_s, kend_s,
                 a_s, qd, kv, dec, st,
                 *, seq, n_lb_rows, layer):
    C = CHUNK
    n_chunks = seq // C
    h = pl.program_id(1)

    def lower_bound(d):
        rows = [lbl_ref[d * n_lb_rows + r:d * n_lb_rows + r + 1, :] for r in range(n_lb_rows)]
        m = functools.reduce(jnp.maximum, rows)
        e = [jnp.exp(r - m) for r in rows]
        return sum(e[:layer + 1]) / sum(e)

    lb_f = lower_bound(0)
    lb_b = lower_bound(1)

    @pl.when(h == 0)
    def _():
        def norm_body(r, carry):
            rows = pl.ds(pl.multiple_of(r * NORM_ROWS, NORM_ROWS), NORM_ROWS)
            u_s[rows, :] = _rmsnorm_bf16(x_ref[rows, :], gmix_ref[...])
            return carry

        lax.fori_loop(0, seq // NORM_ROWS, norm_body, 0, unroll=2)

    for i, w_ref in enumerate((wq_ref, wff_ref, wfb_ref, wi_ref, wog_ref, wp_ref)):
        w_s[:, i * HEAD_DIM:(i + 1) * HEAD_DIM] = w_ref[...]
    norm_g = ng_ref[...]

    def proj_block(start, size):
        rows = pl.ds(start, size)
        raw_s[rows, :] = jnp.dot(u_s[rows, :], w_s[...], preferred_element_type=F32)

    def activation_block(start, size):
        rows = pl.ds(start, size)
        pr = raw_s[rows, :]
        q = pr[:, 0:128]
        qs[rows, :] = q * _sigmoid(q)
        f_f = lb_f + (1.0 - lb_f) * _sigmoid(pr[:, 128:256])
        kf[rows, :] = 1.0 - f_f
        f_b = lb_b + (1.0 - lb_b) * _sigmoid(pr[:, 256:384])
        kb[rows, :] = 1.0 - f_b
        for d, f in enumerate((f_f, f_b)):
            g = jnp.log(f) * LOG2_E
            hi = g.astype(BF16)
            lo = (g - hi.astype(F32)).astype(BF16)
            for m in range(size // (2 * C)):
                c0, c1 = slice(2 * m * C, (2 * m + 1) * C), slice((2 * m + 1) * C, (2 * m + 2) * C)
                g_s[d, start // (2 * C) + m] = jnp.concatenate(
                    [jnp.concatenate([hi[c0], hi[c1]], axis=1),
                     jnp.concatenate([lo[c0], lo[c1]], axis=1)], axis=0)
        vb[rows, :] = pr[:, 384:512].astype(BF16)
        og = pr[:, 512:640]
        ogs[rows, :] = og * _sigmoid(og) * norm_g
        pbuf[pl.ds(start + C, size), :] = _split_hi_lo(pr[:, 640:768])

    pbuf[0:C, :] = jnp.zeros((C, 2 * HEAD_DIM), BF16)
    pbuf[seq + C:seq + 2 * C, :] = jnp.zeros((C, 2 * HEAD_DIM), BF16)

    n_blk = C // BLOCK

    def prefix_pair(m):
        for d in range(2):
            pre = jnp.dot(tri_ref[d], g_s[d, m], preferred_element_type=F32)
            pre_s[d, 2 * m] = pre[:, 0:HEAD_DIM]
            pre_s[d, 2 * m + 1] = pre[:, HEAD_DIM:2 * HEAD_DIM]

    def operand_body(j, carry):
        rows = pl.ds(_aligned(j * C, C), C)
        q = qs[rows, :]
        qd_parts, kend_parts, dec_parts = [], [], []
        q32, k32, q64, k64 = {}, {}, {}, {}
        for d, k_ref in enumerate((kf, kb)):
            fwd = d == 0
            k = k_ref[rows, :]
            pre = pre_s[d, j]

            def anchor(r, d=d):
                return pre_s[d, j, r:r + 1, :]

            a_64 = anchor(C // 2 - 1 if fwd else C // 2)
            a_end = anchor(C - 1 if fwd else 0)
            qc, kc, qdb, keb = [], [], [], []
            for b in range(n_blk):
                lo = b * BLOCK
                pre_b, q_b, k_b = pre[lo:lo + BLOCK], q[lo:lo + BLOCK], k[lo:lo + BLOCK]
                mid = anchor(lo + BLOCK // 2 - 1 if fwd else lo + BLOCK // 2)
                e_mid = pre_b - mid
                qc_b = (q_b * jnp.exp2(e_mid)).astype(BF16)
                kc_b = (k_b * jnp.exp2(-e_mid)).astype(BF16)
                qc.append(qc_b)
                kc.append(kc_b)

                def factor(e):
                    return jnp.broadcast_to(jnp.exp2(e), (BLOCK, HEAD_DIM)).astype(BF16)

                if (b % 2 == 1) == fwd:
                    q32[b] = qc_b * factor(mid - anchor(lo - 1 if fwd else lo + BLOCK))
                else:
                    k32[b] = kc_b * factor(anchor(lo + BLOCK - 1 if fwd else lo) - mid)
                if (b >= n_blk // 2) == fwd:
                    q64[b] = qc_b * factor(mid - a_64)
                else:
                    k64[b] = kc_b * factor(a_64 - mid)
                qdb.append(qc_b * factor(mid))
                keb.append(kc_b * factor(a_end - mid))
            qc_s[d, rows, :] = jnp.concatenate(qc, axis=0)
            kc_s[d, rows, :] = jnp.concatenate(kc, axis=0)
            qd_parts.append(jnp.concatenate(qdb, axis=0))
            kend_parts.append(jnp.concatenate(keb, axis=0))
            dec_parts.append(jnp.exp2(a_end))
        stack = lambda blocks: jnp.concatenate([blocks[b] for b in range(n_blk)], axis=0)
        xq_s[rows, :] = jnp.concatenate([stack(q32), stack(q64)], axis=1)
        x32k_s[rows, :] = stack(k32)
        x64k_s[rows, :] = stack(k64)
        qd[rows, :] = jnp.concatenate(qd_parts, axis=1)
        kend_s[rows, :] = jnp.concatenate(kend_parts, axis=1)
        dec[j] = jnp.broadcast_to(jnp.concatenate(dec_parts, axis=1), (8, 2 * HEAD_DIM))
        return carry

    def chunk_body(j, carry):
        rows = pl.ds(_aligned(j * C, C), C)
        def paired_nt(x_ab, y_a, y_b):
            zero_c = jnp.zeros((C, HEAD_DIM), BF16)
            y_diag = jnp.concatenate([jnp.concatenate([y_a, zero_c], axis=1),
                                      jnp.concatenate([zero_c, y_b], axis=1)], axis=0)
            p = lax.dot_general(x_ab, y_diag, _NT, preferred_element_type=F32)
            return p[:, 0:C], p[:, C:2 * C]

        p_f, p_b = paired_nt(jnp.concatenate([qc_s[0, rows, :], qc_s[1, rows, :]], axis=1),
                             kc_s[0, rows, :], kc_s[1, rows, :])
        a_near = jnp.where(dmask_ref[0] != 0, p_f, 0.0) + jnp.where(dmask_ref[1] != 0, p_b, 0.0)
        p_32, p_64 = paired_nt(xq_s[rows, :], x32k_s[rows, :], x64k_s[rows, :])
        a = jnp.where(dmask_ref[2] != 0, p_64, jnp.where(dmask_ref[3] != 0, a_near, p_32))
        a_s[rows, :] = a.astype(BF16)
        kv[j] = lax.dot_general(vb[rows, :], kend_s[rows, :], _TN, preferred_element_type=F32)
        return carry

    sizes = [EDGE_ROWS] + [PROJ_ROWS] * ((seq - 2 * EDGE_ROWS) // PROJ_ROWS) + [EDGE_ROWS]
    blocks = [(sum(sizes[:i]), size) for i, size in enumerate(sizes)]

    def local_block(start, size):
        activation_block(start, size)
        for m in range(size // (2 * C)):
            prefix_pair(start // (2 * C) + m)
        for c in range(size // C):
            j = start // C + c
            chunk_body(j, operand_body(j, 0))

    proj_block(*blocks[0])
    for prev, cur in zip(blocks[:-1], blocks[1:]):
        proj_block(*cur)
        local_block(*prev)
    local_block(*blocks[-1])

    def fwd_body(j, s):
        st[j, :, 0:HEAD_DIM] = s.astype(BF16)
        return s * dec[j, 0:1, 0:HEAD_DIM] + kv[j, :, 0:HEAD_DIM]

    lax.fori_loop(0, n_chunks, fwd_body, jnp.zeros((HEAD_DIM, HEAD_DIM), F32))

    def bwd_body(i, s):
        j = n_chunks - 1 - i
        st[j, :, HEAD_DIM:2 * HEAD_DIM] = s.astype(BF16)
        return s * dec[j, 0:1, HEAD_DIM:2 * HEAD_DIM] + kv[j, :, HEAD_DIM:2 * HEAD_DIM]

    lax.fori_loop(0, n_chunks, bwd_body, jnp.zeros((HEAD_DIM, HEAD_DIM), F32))

    grp = h // (HEADS // len(POOL_WINDOWS))
    half = sum(jnp.where(grp == i, w // 2, 0) for i, w in enumerate(POOL_WINDOWS))
    t_loc = lax.broadcasted_iota(jnp.int32, (C, C + 2 * HALO), 0)
    s_ext = lax.broadcasted_iota(jnp.int32, (C, C + 2 * HALO), 1)
    off = s_ext - HALO - t_loc
    band = jnp.where((off >= 1 - half) & (off <= half), 1.0, 0.0).astype(BF16)
    inv_w = sum(jnp.where(grp == i, 1.0 / w, 0.0) for i, w in enumerate(POOL_WINDOWS))

    def out_chunk(j, o_intra, clipped):
        start = j * C
        rows = pl.ds(start, C)
        o = o_intra + lax.dot_general(qd[rows, :], st[j], _NT, preferred_element_type=F32)
        ms = jnp.mean(o * o, axis=-1, keepdims=True)
        ya_ref[rows, :] = (o * lax.rsqrt(ms + RMS_EPS) * ogs[rows, :]).astype(ya_ref.dtype)
        ws2 = jnp.dot(band, pbuf[pl.ds(start + C - HALO, C + 2 * HALO), :], preferred_element_type=F32)
        ws = ws2[:, 0:128] + ws2[:, 128:256]
        if clipped:
            t = j * C + lax.broadcasted_iota(jnp.int32, (C, HEAD_DIM), 0)
            cnt = jnp.minimum(t + half + 1, seq) - jnp.maximum(t - half + 1, 0)
            mean = ws / cnt.astype(F32)
        else:
            mean = ws * inv_w
        yp_ref[rows, :] = (mean - raw_s[rows, 5 * HEAD_DIM:6 * HEAD_DIM]).astype(yp_ref.dtype)

    zero_v = jnp.zeros((C, HEAD_DIM), BF16)
    for m in range(n_chunks // 2):
        r0, r1 = pl.ds(2 * m * C, C), pl.ds((2 * m + 1) * C, C)
        v_diag = jnp.concatenate([jnp.concatenate([vb[r0, :], zero_v], axis=1),
                                  jnp.concatenate([zero_v, vb[r1, :]], axis=1)], axis=0)
        o_pair = jnp.dot(jnp.concatenate([a_s[r0, :], a_s[r1, :]], axis=1), v_diag,
                         preferred_element_type=F32)
        for i in range(2):
            j = 2 * m + i
            out_chunk(j, o_pair[:, i * HEAD_DIM:(i + 1) * HEAD_DIM], clipped=j in (0, n_chunks - 1))


def _hgrn_pool(x3, g_mix, w_l, lb_rows, norm_g, tri, dmask, layer):
    B, S, D = x3.shape
    n_lb_rows = lb_rows.shape[0] // 2
    n_chunks = S // CHUNK
    row_f32 = lambda: pltpu.VMEM((S, HEAD_DIM), F32)
    dir_bf16 = lambda: pltpu.VMEM((2, S, HEAD_DIM), BF16)
    out_spec = pl.BlockSpec((None, S, HEAD_DIM), lambda b, h: (b, 0, h))
    return pl.pallas_call(
        functools.partial(_hgrn_kernel, seq=S, n_lb_rows=n_lb_rows, layer=layer),
        grid=(B, HEADS),
        in_specs=[
            pl.BlockSpec((None, S, D), lambda b, h: (b, 0, 0)),
            pl.BlockSpec((1, D), lambda b, h: (0, 0)),
            *[pl.BlockSpec((D, HEAD_DIM), functools.partial(lambda b, h, i: (0, i * HEADS + h), i=i))
              for i in range(6)],
            pl.BlockSpec((2 * n_lb_rows, HEAD_DIM), lambda b, h: (0, h)),
            pl.BlockSpec((1, HEAD_DIM), lambda b, h: (0, h)),
            pl.BlockSpec((2, CHUNK, 2 * CHUNK), lambda b, h: (0, 0, 0)),
            pl.BlockSpec((4, CHUNK, CHUNK), lambda b, h: (0, 0, 0)),
        ],
        out_specs=[out_spec, out_spec],
        out_shape=[jax.ShapeDtypeStruct((B, S, D), BF16), jax.ShapeDtypeStruct((B, S, D), BF16)],
        scratch_shapes=[
            pltpu.VMEM((S, D), BF16),
            pltpu.VMEM((D, 6 * HEAD_DIM), BF16),
            pltpu.VMEM((S, 6 * HEAD_DIM), F32),
            row_f32(), row_f32(), row_f32(),
            pltpu.VMEM((2, n_chunks // 2, 2 * CHUNK, 2 * HEAD_DIM), BF16),
            pltpu.VMEM((S, HEAD_DIM), BF16),
            row_f32(),
            pltpu.VMEM((S + 2 * CHUNK, 2 * HEAD_DIM), BF16),
            pltpu.VMEM((2, n_chunks, CHUNK, HEAD_DIM), F32),
            dir_bf16(), dir_bf16(),
            pltpu.VMEM((S, 2 * HEAD_DIM), BF16),
            pltpu.VMEM((S, HEAD_DIM), BF16), pltpu.VMEM((S, HEAD_DIM), BF16),
            pltpu.VMEM((S, 2 * HEAD_DIM), BF16),
            pltpu.VMEM((S, CHUNK), BF16),
            pltpu.VMEM((S, 2 * HEAD_DIM), BF16),
            pltpu.VMEM((n_chunks, HEAD_DIM, 2 * HEAD_DIM), F32),
            pltpu.VMEM((n_chunks, 8, 2 * HEAD_DIM), F32),
            pltpu.VMEM((n_chunks, HEAD_DIM, 2 * HEAD_DIM), BF16),
        ],
        compiler_params=pltpu.CompilerParams(
            dimension_semantics=("arbitrary", "arbitrary"),
            vmem_limit_bytes=V7X_VMEM_LIMIT),
        name="hgrn_pool",
    )(x3, g_mix, *([w_l] * 6), lb_rows, norm_g, tri, dmask)


def _merge_kernel(x_ref, gmix_ref, ya_ref, yp_ref, wg_ref, pw_ref, ps_ref, wa_ref, wb_ref, wo_ref, h_ref):
    D = x_ref.shape[1]
    n_grp = pw_ref.shape[0]
    gw = D // n_grp
    u = _rmsnorm_bf16(x_ref[...], gmix_ref[...])
    gates = jnp.dot(u, wg_ref[...], preferred_element_type=F32)
    yb = jnp.concatenate(
        [jnp.dot(yp_ref[:, g * gw:(g + 1) * gw], pw_ref[g], preferred_element_type=F32)
         for g in range(n_grp)], axis=1)
    yb = (yb * ps_ref[...]).astype(BF16)
    za = jnp.dot(ya_ref[...], wa_ref[...], preferred_element_type=F32)
    zb = jnp.dot(yb, wb_ref[...], preferred_element_type=F32)
    merged = _sigmoid(gates[:, 0:D]) * za + _sigmoid(gates[:, D:2 * D]) * zb
    h_ref[...] = x_ref[...] + jnp.dot(merged.astype(BF16), wo_ref[...], preferred_element_type=F32)


def _const_spec(shape):
    return pl.BlockSpec(shape, lambda i: (0,) * len(shape), pipeline_mode=pl.Buffered(1))


def _merge(x2, g_mix, ya2, yp2, w_l, pool_w, pool_scale, w_a, w_b, w_o, tile):
    T, D = x2.shape
    tok = lambda: pl.BlockSpec((tile, D), lambda i: (i, 0))
    return pl.pallas_call(
        _merge_kernel,
        grid=(T // tile,),
        in_specs=[tok(), _const_spec((1, D)), tok(), tok(),
                  pl.BlockSpec((D, 2 * D), lambda i: (0, w_l.shape[1] // (2 * D) - 1),
                               pipeline_mode=pl.Buffered(1)),
                  _const_spec(pool_w.shape), _const_spec((1, D)),
                  _const_spec(w_a.shape), _const_spec(w_b.shape), _const_spec(w_o.shape)],
        out_specs=tok(),
        out_shape=jax.ShapeDtypeStruct((T, D), F32),
        compiler_params=pltpu.CompilerParams(
            dimension_semantics=("arbitrary",), vmem_limit_bytes=V7X_VMEM_LIMIT),
        name="merge",
    )(x2, g_mix, ya2, yp2, w_l, pool_w, pool_scale.reshape(1, D), w_a, w_b, w_o)


def _ffn_kernel(h_ref, g_ref, wi_ref, wo_ref, gfin_ref, o_ref, *, final_norm):
    h = h_ref[...]
    u = _rmsnorm_bf16(h, g_ref[...])
    d_ff = wo_ref.shape[0]
    gate = jnp.dot(u, wi_ref[:, 0:d_ff], preferred_element_type=F32)
    up = jnp.dot(u, wi_ref[:, d_ff:2 * d_ff], preferred_element_type=F32)
    act = (gate * _sigmoid(gate) * up).astype(BF16)
    acc = h + jnp.dot(act, wo_ref[...], preferred_element_type=F32)
    if final_norm:
        ms = jnp.mean(acc * acc, axis=-1, keepdims=True)
        acc = acc * lax.rsqrt(ms + RMS_EPS) * gfin_ref[...]
    o_ref[...] = acc


def _ffn(h2, g_ffn, w_in, w_out, g_final, tile, final_norm):
    T, D = h2.shape
    tok = lambda: pl.BlockSpec((tile, D), lambda i: (i, 0))
    return pl.pallas_call(
        functools.partial(_ffn_kernel, final_norm=final_norm),
        grid=(T // tile,),
        in_specs=[tok(), _const_spec((1, D)), _const_spec(w_in.shape), _const_spec(w_out.shape),
                  _const_spec((1, D))],
        out_specs=tok(),
        out_shape=jax.ShapeDtypeStruct((T, D), F32),
        compiler_params=pltpu.CompilerParams(
            dimension_semantics=("arbitrary",), vmem_limit_bytes=V7X_VMEM_LIMIT),
        name="ffn",
    )(h2, g_ffn.reshape(1, D), w_in, w_out, g_final.reshape(1, D))


def kernel(x, g_mix, w_in, lb_logits, hgrn_norm_g, pool_w, pool_scale, w_branch_a, w_branch_b,
           w_out, g_ffn, w_ffn_in, w_ffn_out, g_final):
    B, S, D = x.shape
    depth = w_in.shape[0]
    T = B * S
    assert D == HEADS * HEAD_DIM and (S - 2 * EDGE_ROWS) % PROJ_ROWS == 0 and S >= 2 * EDGE_ROWS and max(POOL_WINDOWS) // 2 <= HALO
    assert w_ffn_out.shape[1] % MXU_WIDTH == 0
    assert w_in.shape[2] == 8 * D
    tri_np, dmask_np = _chunk_constants()
    tri = jnp.asarray(tri_np, BF16)
    dmask = jnp.asarray(dmask_np, jnp.int32)
    lb_rows = lb_logits.astype(F32).reshape(2 * (depth + 1), D)

    h2 = x.reshape(T, D)
    for l in range(depth):
        w_l = w_in[l].astype(BF16)
        g_mix_l = g_mix[l].reshape(1, D)
        ya, yp = _hgrn_pool(h2.reshape(B, S, D), g_mix_l, w_l, lb_rows, hgrn_norm_g[l].reshape(1, D),
                            tri, dmask, layer=l)
        h2 = _merge(h2, g_mix_l, ya.reshape(T, D), yp.reshape(T, D), w_l,
                    pool_w[l].astype(BF16), pool_scale[l], w_branch_a[l].astype(BF16),
                    w_branch_b[l].astype(BF16), w_out[l].astype(BF16), tile=512)
        h2 = _ffn(h2, g_ffn[l], w_ffn_in[l].astype(BF16), w_ffn_out[l].astype(BF16), g_final,
                  tile=512, final_norm=(l == depth - 1))
    return h2.reshape(B, S, D)
```

```python
import functools
import math

import numpy as np
import jax
import jax.numpy as jnp
from jax import lax
from jax.experimental import pallas as pl
from jax.experimental.pallas import tpu as pltpu

F32 = jnp.float32
BF16 = jnp.bfloat16

RMS_EPS = 1e-6
HEADS = 8
HEAD_DIM = 128
POOL_WINDOWS = (2, 4, 8, 16)
CHUNK = 128
BLOCK = 32
HALO = 16
NORM_ROWS = 128
PROJ_ROWS = 512
EDGE_ROWS = 256
LOG2_E = 1.0 / math.log(2.0)
V7X_VMEM_LIMIT = 56 * 1024 * 1024
MXU_WIDTH = 256

_TN = (((0,), (0,)), ((), ()))


def _sigmoid(x):
    return 1.0 / (1.0 + jnp.exp(-x))


def _chunk_constants():
    t = np.arange(CHUNK)[:, None]
    s = np.arange(CHUNK)[None, :]
    tri = np.stack([np.tile(s <= t, (1, 2)), np.tile(s >= t, (1, 2))]).astype(np.float32)
    same = (t // BLOCK) == (s // BLOCK)
    cross = (t >= CHUNK // 2) != (s >= CHUNK // 2)
    dmask = np.stack([same & (s <= t), same & (s >= t), cross, same]).astype(np.int32)
    return tri, dmask


def _split_hi_lo(a):
    hi = a.astype(BF16)
    lo = (a - hi.astype(F32)).astype(BF16)
    return jnp.concatenate([hi, lo], axis=1)


def _rmsnorm_bf16(x, g):
    ms = jnp.mean(x * x, axis=-1, keepdims=True)
    return (x * lax.rsqrt(ms + RMS_EPS) * g).astype(BF16)


def _hgrn_kernel(x_ref, gmix_ref, wq_ref, wff_ref, wfb_ref, wi_ref, wog_ref, wp_ref, lbl_ref, ng_ref, tri_ref,
                 dmask_ref, ya_ref, yp_ref, u_s, w_s, raw_s, qs, kf, kb, g_s, vb, ogs, pbuf, pre_s, qc_s, kct_s,
                 xq_s, x32kt_s, x64kt_s, kend_s, a_s, qd, kv, dec, stt, st,
                 *, seq, n_lb_rows, layer):
    C = CHUNK
    n_chunks = seq // C
    h = pl.program_id(1)

    def lower_bound(d):
        rows = [lbl_ref[d * n_lb_rows + r:d * n_lb_rows + r + 1, :] for r in range(n_lb_rows)]
        m = functools.reduce(jnp.maximum, rows)
        e = [jnp.exp(r - m) for r in rows]
        return sum(e[:layer + 1]) / sum(e)

    lb_f = lower_bound(0)
    lb_b = lower_bound(1)

    @pl.when(h == 0)
    def _():
        def norm_body(r, carry):
            rows = pl.ds(pl.multiple_of(r * NORM_ROWS, NORM_ROWS), NORM_ROWS)
            u_s[rows, :] = _rmsnorm_bf16(x_ref[rows, :], gmix_ref[...])
            return carry

        lax.fori_loop(0, seq // NORM_ROWS, norm_body, 0, unroll=2)

    for i, w_ref in enumerate((wq_ref, wff_ref, wfb_ref, wi_ref, wog_ref, wp_ref)):
        w_s[:, i * HEAD_DIM:(i + 1) * HEAD_DIM] = w_ref[...]
    norm_g = ng_ref[...]

    def proj_block(start, size):
        rows = pl.ds(start, size)
        raw_s[rows, :] = jnp.dot(u_s[rows, :], w_s[...], preferred_element_type=F32)

    def activation_block(start, size):
        rows = pl.ds(start, size)
        pr = raw_s[rows, :]
        q = pr[:, 0:128]
        qs[rows, :] = q * _sigmoid(q)
        f_f = lb_f + (1.0 - lb_f) * _sigmoid(pr[:, 128:256])
        kf[rows, :] = 1.0 - f_f
        f_b = lb_b + (1.0 - lb_b) * _sigmoid(pr[:, 256:384])
        kb[rows, :] = 1.0 - f_b
        for d, f in enumerate((f_f, f_b)):
            g = jnp.log(f) * LOG2_E
            hi = g.astype(BF16)
            lo = (g - hi.astype(F32)).astype(BF16)
            for m in range(size // (2 * C)):
                c0, c1 = slice(2 * m * C, (2 * m + 1) * C), slice((2 * m + 1) * C, (2 * m + 2) * C)
                g_s[d, start // (2 * C) + m] = jnp.concatenate(
                    [jnp.concatenate([hi[c0], hi[c1]], axis=1),
                     jnp.concatenate([lo[c0], lo[c1]], axis=1)], axis=0)
        vb[rows, :] = pr[:, 384:512].astype(BF16)
        og = pr[:, 512:640]
        ogs[rows, :] = og * _sigmoid(og) * norm_g
        pbuf[pl.ds(start + C, size), :] = _split_hi_lo(pr[:, 640:768])

    pbuf[0:C, :] = jnp.zeros((C, 2 * HEAD_DIM), BF16)
    pbuf[seq + C:seq + 2 * C, :] = jnp.zeros((C, 2 * HEAD_DIM), BF16)

    n_blk = C // BLOCK

    def prefix_pair(m):
        for d in range(2):
            pre = jnp.dot(tri_ref[d], g_s[d, m], preferred_element_type=F32)
            pre_s[d, 2 * m] = pre[:, 0:HEAD_DIM]
            pre_s[d, 2 * m + 1] = pre[:, HEAD_DIM:2 * HEAD_DIM]

    def operand_chunk(j):
        rows = pl.ds(j * C, C)
        cols = slice(j * C, (j + 1) * C)
        q = qs[rows, :]
        qd_parts, kend_parts, dec_parts = [], [], []
        q32, k32, q64, k64 = {}, {}, {}, {}
        for d, k_ref in enumerate((kf, kb)):
            fwd = d == 0
            k = k_ref[rows, :]
            pre = pre_s[d, j]

            def anchor(r, d=d):
                return pre_s[d, j, r:r + 1, :]

            a_64 = anchor(C // 2 - 1 if fwd else C // 2)
            a_end = anchor(C - 1 if fwd else 0)
            qc, kc, qdb, keb = [], [], [], []
            for b in range(n_blk):
                lo = b * BLOCK
                pre_b, q_b, k_b = pre[lo:lo + BLOCK], q[lo:lo + BLOCK], k[lo:lo + BLOCK]
                mid = anchor(lo + BLOCK // 2 - 1 if fwd else lo + BLOCK // 2)
                e_mid = pre_b - mid
                qc_b = (q_b * jnp.exp2(e_mid)).astype(BF16)
                kc_b = (k_b * jnp.exp2(-e_mid)).astype(BF16)
                qc.append(qc_b)
                kc.append(kc_b)

                def factor(e):
                    return jnp.broadcast_to(jnp.exp2(e), (BLOCK, HEAD_DIM)).astype(BF16)

                if (b % 2 == 1) == fwd:
                    q32[b] = qc_b * factor(mid - anchor(lo - 1 if fwd else lo + BLOCK))
                else:
                    k32[b] = kc_b * factor(anchor(lo + BLOCK - 1 if fwd else lo) - mid)
                if (b >= n_blk // 2) == fwd:
                    q64[b] = qc_b * factor(mid - a_64)
                else:
                    k64[b] = kc_b * factor(a_64 - mid)
                qdb.append(qc_b * factor(mid))
                keb.append(kc_b * factor(a_end - mid))
            qc_s[d, rows, :] = jnp.concatenate(qc, axis=0)
            kct_s[d, :, cols] = jnp.concatenate(kc, axis=0).T
            qd_parts.append(jnp.concatenate(qdb, axis=0))
            kend_parts.append(jnp.concatenate(keb, axis=0))
            dec_parts.append(jnp.exp2(a_end))
        stack = lambda blocks: jnp.concatenate([blocks[b] for b in range(n_blk)], axis=0)
        xq_s[rows, :] = jnp.concatenate([stack(q32), stack(q64)], axis=1)
        x32kt_s[:, cols] = stack(k32).T
        x64kt_s[:, cols] = stack(k64).T
        qd[rows, :] = jnp.concatenate(qd_parts, axis=1)
        kend_s[rows, :] = jnp.concatenate(kend_parts, axis=1)
        dec[j] = jnp.broadcast_to(jnp.concatenate(dec_parts, axis=1), (8, 2 * HEAD_DIM))

    def paired(x_ab, yt_a, yt_b):
        zero_c = jnp.zeros((HEAD_DIM, C), BF16)
        yt_diag = jnp.concatenate([jnp.concatenate([yt_a, zero_c], axis=1),
                                   jnp.concatenate([zero_c, yt_b], axis=1)], axis=0)
        p = jnp.dot(x_ab, yt_diag, preferred_element_type=F32)
        return p[:, 0:C], p[:, C:2 * C]

    def product_chunk(j):
        rows = pl.ds(j * C, C)
        cols = slice(j * C, (j + 1) * C)
        p_f, p_b = paired(jnp.concatenate([qc_s[0, rows, :], qc_s[1, rows, :]], axis=1),
                          kct_s[0, :, cols], kct_s[1, :, cols])
        a_near = jnp.where(dmask_ref[0] != 0, p_f, 0.0) + jnp.where(dmask_ref[1] != 0, p_b, 0.0)
        p_32, p_64 = paired(xq_s[rows, :], x32kt_s[:, cols], x64kt_s[:, cols])
        a = jnp.where(dmask_ref[2] != 0, p_64, jnp.where(dmask_ref[3] != 0, a_near, p_32))
        a_s[rows, :] = a.astype(BF16)
        kv[j] = lax.dot_general(vb[rows, :], kend_s[rows, :], _TN, preferred_element_type=F32)

    sizes = [EDGE_ROWS] + [PROJ_ROWS] * ((seq - 2 * EDGE_ROWS) // PROJ_ROWS) + [EDGE_ROWS]
    blocks = [(sum(sizes[:i]), size) for i, size in enumerate(sizes)]

    def local_block(start, size):
        activation_block(start, size)
        for m in range(size // (2 * C)):
            prefix_pair(start // (2 * C) + m)
        for c in range(size // C):
            operand_chunk(start // C + c)
            product_chunk(start // C + c)

    proj_block(*blocks[0])
    for prev, cur in zip(blocks[:-1], blocks[1:]):
        proj_block(*cur)
        local_block(*prev)
    local_block(*blocks[-1])

    def fwd_body(j, s):
        stt[j, :, 0:HEAD_DIM] = s.astype(BF16)
        return s * dec[j, 0:1, 0:HEAD_DIM] + kv[j, :, 0:HEAD_DIM]

    lax.fori_loop(0, n_chunks, fwd_body, jnp.zeros((HEAD_DIM, HEAD_DIM), F32))

    def bwd_body(i, s):
        j = n_chunks - 1 - i
        stt[j, :, HEAD_DIM:2 * HEAD_DIM] = s.astype(BF16)
        return s * dec[j, 0:1, HEAD_DIM:2 * HEAD_DIM] + kv[j, :, HEAD_DIM:2 * HEAD_DIM]

    lax.fori_loop(0, n_chunks, bwd_body, jnp.zeros((HEAD_DIM, HEAD_DIM), F32))

    for j in range(n_chunks):
        st[j] = stt[j].T

    grp = h // (HEADS // len(POOL_WINDOWS))
    half = sum(jnp.where(grp == i, w // 2, 0) for i, w in enumerate(POOL_WINDOWS))
    t_loc = lax.broadcasted_iota(jnp.int32, (C, C + 2 * HALO), 0)
    s_ext = lax.broadcasted_iota(jnp.int32, (C, C + 2 * HALO), 1)
    off = s_ext - HALO - t_loc
    band = jnp.where((off >= 1 - half) & (off <= half), 1.0, 0.0).astype(BF16)
    inv_w = sum(jnp.where(grp == i, 1.0 / w, 0.0) for i, w in enumerate(POOL_WINDOWS))

    def out_chunk(j, o_intra, clipped):
        start = j * C
        rows = pl.ds(start, C)
        o = o_intra + jnp.dot(qd[rows, :], st[j], preferred_element_type=F32)
        ms = jnp.mean(o * o, axis=-1, keepdims=True)
        ya_ref[rows, :] = (o * lax.rsqrt(ms + RMS_EPS) * ogs[rows, :]).astype(ya_ref.dtype)
        ws2 = jnp.dot(band, pbuf[pl.ds(start + C - HALO, C + 2 * HALO), :], preferred_element_type=F32)
        ws = ws2[:, 0:128] + ws2[:, 128:256]
        if clipped:
            t = j * C + lax.broadcasted_iota(jnp.int32, (C, HEAD_DIM), 0)
            cnt = jnp.minimum(t + half + 1, seq) - jnp.maximum(t - half + 1, 0)
            mean = ws / cnt.astype(F32)
        else:
            mean = ws * inv_w
        yp_ref[rows, :] = (mean - raw_s[rows, 5 * HEAD_DIM:6 * HEAD_DIM]).astype(yp_ref.dtype)

    zero_v = jnp.zeros((C, HEAD_DIM), BF16)
    for m in range(n_chunks // 2):
        r0, r1 = pl.ds(2 * m * C, C), pl.ds((2 * m + 1) * C, C)
        v_diag = jnp.concatenate([jnp.concatenate([vb[r0, :], zero_v], axis=1),
                                  jnp.concatenate([zero_v, vb[r1, :]], axis=1)], axis=0)
        o_pair = jnp.dot(jnp.concatenate([a_s[r0, :], a_s[r1, :]], axis=1), v_diag,
                         preferred_element_type=F32)
        for i in range(2):
            j = 2 * m + i
            out_chunk(j, o_pair[:, i * HEAD_DIM:(i + 1) * HEAD_DIM], clipped=j in (0, n_chunks - 1))


def _hgrn_pool(x3, g_mix, w_l, lb_rows, norm_g, tri, dmask, layer):
    B, S, D = x3.shape
    n_lb_rows = lb_rows.shape[0] // 2
    n_chunks = S // CHUNK
    row_f32 = lambda: pltpu.VMEM((S, HEAD_DIM), F32)
    out_spec = pl.BlockSpec((None, S, HEAD_DIM), lambda b, h: (b, 0, h))
    return pl.pallas_call(
        functools.partial(_hgrn_kernel, seq=S, n_lb_rows=n_lb_rows, layer=layer),
        grid=(B, HEADS),
        in_specs=[
            pl.BlockSpec((None, S, D), lambda b, h: (b, 0, 0)),
            pl.BlockSpec((1, D), lambda b, h: (0, 0)),
            *[pl.BlockSpec((D, HEAD_DIM), functools.partial(lambda b, h, i: (0, i * HEADS + h), i=i))
              for i in range(6)],
            pl.BlockSpec((2 * n_lb_rows, HEAD_DIM), lambda b, h: (0, h)),
            pl.BlockSpec((1, HEAD_DIM), lambda b, h: (0, h)),
            pl.BlockSpec((2, CHUNK, 2 * CHUNK), lambda b, h: (0, 0, 0)),
            pl.BlockSpec((4, CHUNK, CHUNK), lambda b, h: (0, 0, 0)),
        ],
        out_specs=[out_spec, out_spec],
        out_shape=[jax.ShapeDtypeStruct((B, S, D), BF16), jax.ShapeDtypeStruct((B, S, D), BF16)],
        scratch_shapes=[
            pltpu.VMEM((S, D), BF16),
            pltpu.VMEM((D, 6 * HEAD_DIM), BF16),
            pltpu.VMEM((S, 6 * HEAD_DIM), F32),
            row_f32(), row_f32(), row_f32(),
            pltpu.VMEM((2, n_chunks // 2, 2 * CHUNK, 2 * HEAD_DIM), BF16),
            pltpu.VMEM((S, HEAD_DIM), BF16),
            row_f32(),
            pltpu.VMEM((S + 2 * CHUNK, 2 * HEAD_DIM), BF16),
            pltpu.VMEM((2, n_chunks, CHUNK, HEAD_DIM), F32),
            pltpu.VMEM((2, S, HEAD_DIM), BF16),
            pltpu.VMEM((2, HEAD_DIM, S), BF16),
            pltpu.VMEM((S, 2 * HEAD_DIM), BF16),
            pltpu.VMEM((HEAD_DIM, S), BF16), pltpu.VMEM((HEAD_DIM, S), BF16),
            pltpu.VMEM((S, 2 * HEAD_DIM), BF16),
            pltpu.VMEM((S, CHUNK), BF16),
            pltpu.VMEM((S, 2 * HEAD_DIM), BF16),
            pltpu.VMEM((n_chunks, HEAD_DIM, 2 * HEAD_DIM), F32),
            pltpu.VMEM((n_chunks, 8, 2 * HEAD_DIM), F32),
            pltpu.VMEM((n_chunks, HEAD_DIM, 2 * HEAD_DIM), BF16),
            pltpu.VMEM((n_chunks, 2 * HEAD_DIM, HEAD_DIM), BF16),
        ],
        compiler_params=pltpu.CompilerParams(
            dimension_semantics=("arbitrary", "arbitrary"),
            vmem_limit_bytes=V7X_VMEM_LIMIT),
        name="hgrn_pool",
    )(x3, g_mix, *([w_l] * 6), lb_rows, norm_g, tri, dmask)


def _merge_kernel(x_ref, gmix_ref, ya_ref, yp_ref, wg_ref, pw_ref, ps_ref, wa_ref, wb_ref, wo_ref, h_ref):
    D = x_ref.shape[1]
    n_grp = pw_ref.shape[0]
    gw = D // n_grp
    u = _rmsnorm_bf16(x_ref[...], gmix_ref[...])
    gates = jnp.dot(u, wg_ref[...], preferred_element_type=F32)
    yb = jnp.concatenate(
        [jnp.dot(yp_ref[:, g * gw:(g + 1) * gw], pw_ref[g], preferred_element_type=F32)
         for g in range(n_grp)], axis=1)
    yb = (yb * ps_ref[...]).astype(BF16)
    za = jnp.dot(ya_ref[...], wa_ref[...], preferred_element_type=F32)
    zb = jnp.dot(yb, wb_ref[...], preferred_element_type=F32)
    merged = _sigmoid(gates[:, 0:D]) * za + _sigmoid(gates[:, D:2 * D]) * zb
    h_ref[...] = x_ref[...] + jnp.dot(merged.astype(BF16), wo_ref[...], preferred_element_type=F32)


def _const_spec(shape):
    return pl.BlockSpec(shape, lambda i: (0,) * len(shape), pipeline_mode=pl.Buffered(1))


def _merge(x2, g_mix, ya2, yp2, w_l, pool_w, pool_scale, w_a, w_b, w_o, tile):
    T, D = x2.shape
    tok = lambda: pl.BlockSpec((tile, D), lambda i: (i, 0))
    return pl.pallas_call(
        _merge_kernel,
        grid=(T // tile,),
        in_specs=[tok(), _const_spec((1, D)), tok(), tok(),
                  pl.BlockSpec((D, 2 * D), lambda i: (0, w_l.shape[1] // (2 * D) - 1),
                               pipeline_mode=pl.Buffered(1)),
                  _const_spec(pool_w.shape), _const_spec((1, D)),
                  _const_spec(w_a.shape), _const_spec(w_b.shape), _const_spec(w_o.shape)],
        out_specs=tok(),
        out_shape=jax.ShapeDtypeStruct((T, D), F32),
        compiler_params=pltpu.CompilerParams(
            dimension_semantics=("arbitrary",), vmem_limit_bytes=V7X_VMEM_LIMIT),
        name="merge",
    )(x2, g_mix, ya2, yp2, w_l, pool_w, pool_scale.reshape(1, D), w_a, w_b, w_o)


def _ffn_kernel(h_ref, g_ref, wi_ref, wo_ref, gfin_ref, o_ref, *, final_norm):
    h = h_ref[...]
    u = _rmsnorm_bf16(h, g_ref[...])
    d_ff = wo_ref.shape[0]
    gate = jnp.dot(u, wi_ref[:, 0:d_ff], preferred_element_type=F32)
    up = jnp.dot(u, wi_ref[:, d_ff:2 * d_ff], preferred_element_type=F32)
    act = (gate * _sigmoid(gate) * up).astype(BF16)
    acc = h + jnp.dot(act, wo_ref[...], preferred_element_type=F32)
    if final_norm:
        ms = jnp.mean(acc * acc, axis=-1, keepdims=True)
        acc = acc * lax.rsqrt(ms + RMS_EPS) * gfin_ref[...]
    o_ref[...] = acc


def _ffn(h2, g_ffn, w_in, w_out, g_final, tile, final_norm):
    T, D = h2.shape
    tok = lambda: pl.BlockSpec((tile, D), lambda i: (i, 0))
    return pl.pallas_call(
        functools.partial(_ffn_kernel, final_norm=final_norm),
        grid=(T // tile,),
        in_specs=[tok(), _const_spec((1, D)), _const_spec(w_in.shape), _const_spec(w_out.shape),
                  _const_spec((1, D))],
        out_specs=tok(),
        out_shape=jax.ShapeDtypeStruct((T, D), F32),
        compiler_params=pltpu.CompilerParams(
            dimension_semantics=("arbitrary",), vmem_limit_bytes=V7X_VMEM_LIMIT),
        name="ffn",
    )(h2, g_ffn.reshape(1, D), w_in, w_out, g_final.reshape(1, D))


def kernel(x, g_mix, w_in, lb_logits, hgrn_norm_g, pool_w, pool_scale, w_branch_a, w_branch_b,
           w_out, g_ffn, w_ffn_in, w_ffn_out, g_final):
    B, S, D = x.shape
    depth = w_in.shape[0]
    T = B * S
    assert D == HEADS * HEAD_DIM
    assert (S - 2 * EDGE_ROWS) % PROJ_ROWS == 0 and S >= 2 * EDGE_ROWS and max(POOL_WINDOWS) // 2 <= HALO
    assert w_ffn_out.shape[1] % MXU_WIDTH == 0
    assert w_in.shape[2] == 8 * D
    tri_np, dmask_np = _chunk_constants()
    tri = jnp.asarray(tri_np, BF16)
    dmask = jnp.asarray(dmask_np, jnp.int32)
    lb_rows = lb_logits.astype(F32).reshape(2 * (depth + 1), D)

    h2 = x.reshape(T, D)
    for l in range(depth):
        w_l = w_in[l].astype(BF16)
        g_mix_l = g_mix[l].reshape(1, D)
        ya, yp = _hgrn_pool(h2.reshape(B, S, D), g_mix_l, w_l, lb_rows, hgrn_norm_g[l].reshape(1, D),
                            tri, dmask, layer=l)
        h2 = _merge(h2, g_mix_l, ya.reshape(T, D), yp.reshape(T, D), w_l,
                    pool_w[l].astype(BF16), pool_scale[l], w_branch_a[l].astype(BF16),
                    w_branch_b[l].astype(BF16), w_out[l].astype(BF16), tile=512)
        h2 = _ffn(h2, g_ffn[l], w_ffn_in[l].astype(BF16), w_ffn_out[l].astype(BF16), g_final,
                  tile=512, final_norm=(l == depth - 1))
    return h2.reshape(B, S, D)
```

```python
import functools
import math

import numpy as np
import jax
import jax.numpy as jnp
from jax import lax
from jax.experimental import pallas as pl
from jax.experimental.pallas import tpu as pltpu

F32 = jnp.float32
BF16 = jnp.bfloat16

RMS_EPS = 1e-6
HEADS = 8
HEAD_DIM = 128
POOL_WINDOWS = (2, 4, 8, 16)
CHUNK = 128
BLOCK = 32
HALO = 16
NORM_ROWS = 128
PROJ_ROWS = 768
EDGE_ROWS = 256
LOG2_E = 1.0 / math.log(2.0)
V7X_VMEM_LIMIT = 56 * 1024 * 1024
MXU_WIDTH = 256

_TN = (((0,), (0,)), ((), ()))


def _sigmoid(x):
    return 1.0 / (1.0 + jnp.exp(-x))


def _chunk_constants():
    t = np.arange(CHUNK)[:, None]
    s = np.arange(CHUNK)[None, :]
    tri = np.stack([np.tile(s <= t, (1, 2)), np.tile(s >= t, (1, 2))]).astype(np.float32)
    same = (t // BLOCK) == (s // BLOCK)
    cross = (t >= CHUNK // 2) != (s >= CHUNK // 2)
    dmask = np.stack([same & (s <= t), same & (s >= t), cross, same]).astype(np.int32)
    return tri, dmask


def _split_hi_lo(a):
    hi = a.astype(BF16)
    lo = (a - hi.astype(F32)).astype(BF16)
    return jnp.concatenate([hi, lo], axis=1)


def _rmsnorm_bf16(x, g):
    ms = jnp.mean(x * x, axis=-1, keepdims=True)
    return (x * lax.rsqrt(ms + RMS_EPS) * g).astype(BF16)


def _hgrn_kernel(x_ref, gmix_ref, wq_ref, wff_ref, wfb_ref, wi_ref, wog_ref, wp_ref, lbl_ref, ng_ref, tri_ref,
                 dmask_ref, ya_ref, yp_ref, u_s, w_s, raw_s, qs, kf, kb, g_s, vb, ogs, pbuf, pre_s, qc_s, kct_s,
                 xq_s, x32kt_s, x64kt_s, kend_s, a_s, qd, kv, dec, stt, st,
                 *, seq, n_lb_rows, layer):
    C = CHUNK
    n_chunks = seq // C
    h = pl.program_id(1)

    def lower_bound(d):
        rows = [lbl_ref[d * n_lb_rows + r:d * n_lb_rows + r + 1, :] for r in range(n_lb_rows)]
        m = functools.reduce(jnp.maximum, rows)
        e = [jnp.exp(r - m) for r in rows]
        return sum(e[:layer + 1]) / sum(e)

    lb_f = lower_bound(0)
    lb_b = lower_bound(1)

    def run_head(with_norm):
        for i, w_ref in enumerate((wq_ref, wff_ref, wfb_ref, wi_ref, wog_ref, wp_ref)):
            w_s[:, i * HEAD_DIM:(i + 1) * HEAD_DIM] = w_ref[...]
        norm_g = ng_ref[...]

        def proj_block(start, size):
            rows = pl.ds(start, size)
            if with_norm:
                for r0 in range(start, start + size, NORM_ROWS):
                    nr = pl.ds(r0, NORM_ROWS)
                    u_s[nr, :] = _rmsnorm_bf16(x_ref[nr, :], gmix_ref[...])
            raw_s[rows, :] = jnp.dot(u_s[rows, :], w_s[...], preferred_element_type=F32)

        def activation_block(start, size):
            rows = pl.ds(start, size)
            pr = raw_s[rows, :]
            q = pr[:, 0:128]
            qs[rows, :] = q * _sigmoid(q)
            f_f = lb_f + (1.0 - lb_f) * _sigmoid(pr[:, 128:256])
            kf[rows, :] = 1.0 - f_f
            f_b = lb_b + (1.0 - lb_b) * _sigmoid(pr[:, 256:384])
            kb[rows, :] = 1.0 - f_b
            for d, f in enumerate((f_f, f_b)):
                g = jnp.log(f) * LOG2_E
                hi = g.astype(BF16)
                lo = (g - hi.astype(F32)).astype(BF16)
                for m in range(size // (2 * C)):
                    c0, c1 = slice(2 * m * C, (2 * m + 1) * C), slice((2 * m + 1) * C, (2 * m + 2) * C)
                    g_s[d, start // (2 * C) + m] = jnp.concatenate(
                        [jnp.concatenate([hi[c0], hi[c1]], axis=1),
                         jnp.concatenate([lo[c0], lo[c1]], axis=1)], axis=0)
            vb[rows, :] = pr[:, 384:512].astype(BF16)
            og = pr[:, 512:640]
            ogs[rows, :] = og * _sigmoid(og) * norm_g
            pbuf[pl.ds(start + C, size), :] = _split_hi_lo(pr[:, 640:768])

        pbuf[0:C, :] = jnp.zeros((C, 2 * HEAD_DIM), BF16)
        pbuf[seq + C:seq + 2 * C, :] = jnp.zeros((C, 2 * HEAD_DIM), BF16)

        n_blk = C // BLOCK

        def prefix_pair(m):
            for d in range(2):
                pre = jnp.dot(tri_ref[d], g_s[d, m], preferred_element_type=F32)
                pre_s[d, 2 * m] = pre[:, 0:HEAD_DIM]
                pre_s[d, 2 * m + 1] = pre[:, HEAD_DIM:2 * HEAD_DIM]

        def operand_chunk(j):
            rows = pl.ds(j * C, C)
            cols = slice(j * C, (j + 1) * C)
            q = qs[rows, :]
            qd_parts, kend_parts, dec_parts = [], [], []
            q32, k32, q64, k64 = {}, {}, {}, {}
            for d, k_ref in enumerate((kf, kb)):
                fwd = d == 0
                k = k_ref[rows, :]
                pre = pre_s[d, j]

                def anchor(r, d=d):
                    return pre_s[d, j, r:r + 1, :]

                a_64 = anchor(C // 2 - 1 if fwd else C // 2)
                a_end = anchor(C - 1 if fwd else 0)
                qc, kc, qdb, keb = [], [], [], []
                for b in range(n_blk):
                    lo = b * BLOCK
                    pre_b, q_b, k_b = pre[lo:lo + BLOCK], q[lo:lo + BLOCK], k[lo:lo + BLOCK]
                    mid = anchor(lo + BLOCK // 2 - 1 if fwd else lo + BLOCK // 2)
                    e_mid = pre_b - mid
                    qc_b = (q_b * jnp.exp2(e_mid)).astype(BF16)
                    kc_b = (k_b * jnp.exp2(-e_mid)).astype(BF16)
                    qc.append(qc_b)
                    kc.append(kc_b)

                    def factor(e):
                        return jnp.broadcast_to(jnp.exp2(e), (BLOCK, HEAD_DIM)).astype(BF16)

                    if (b % 2 == 1) == fwd:
                        q32[b] = qc_b * factor(mid - anchor(lo - 1 if fwd else lo + BLOCK))
                    else:
                        k32[b] = kc_b * factor(anchor(lo + BLOCK - 1 if fwd else lo) - mid)
                    if (b >= n_blk // 2) == fwd:
                        q64[b] = qc_b * factor(mid - a_64)
                    else:
                        k64[b] = kc_b * factor(a_64 - mid)
                    qdb.append(qc_b * factor(mid))
                    keb.append(kc_b * factor(a_end - mid))
                qc_s[d, rows, :] = jnp.concatenate(qc, axis=0)
                kct_s[d, :, cols] = jnp.concatenate(kc, axis=0).T
                qd_parts.append(jnp.concatenate(qdb, axis=0))
                kend_parts.append(jnp.concatenate(keb, axis=0))
                dec_parts.append(jnp.exp2(a_end))
            stack = lambda blocks: jnp.concatenate([blocks[b] for b in range(n_blk)], axis=0)
            xq_s[rows, :] = jnp.concatenate([stack(q32), stack(q64)], axis=1)
            x32kt_s[:, cols] = stack(k32).T
            x64kt_s[:, cols] = stack(k64).T
            qd[rows, :] = jnp.concatenate(qd_parts, axis=1)
            kend_s[rows, :] = jnp.concatenate(kend_parts, axis=1)
            dec[j] = jnp.broadcast_to(jnp.concatenate(dec_parts, axis=1), (8, 2 * HEAD_DIM))

        def paired(x_ab, yt_a, yt_b):
            zero_c = jnp.zeros((HEAD_DIM, C), BF16)
            yt_diag = jnp.concatenate([jnp.concatenate([yt_a, zero_c], axis=1),
                                       jnp.concatenate([zero_c, yt_b], axis=1)], axis=0)
            p = jnp.dot(x_ab, yt_diag, preferred_element_type=F32)
            return p[:, 0:C], p[:, C:2 * C]

        def product_chunk(j):
            rows = pl.ds(j * C, C)
            cols = slice(j * C, (j + 1) * C)
            p_f, p_b = paired(jnp.concatenate([qc_s[0, rows, :], qc_s[1, rows, :]], axis=1),
                              kct_s[0, :, cols], kct_s[1, :, cols])
            a_near = jnp.where(dmask_ref[0] != 0, p_f, 0.0) + jnp.where(dmask_ref[1] != 0, p_b, 0.0)
            p_32, p_64 = paired(xq_s[rows, :], x32kt_s[:, cols], x64kt_s[:, cols])
            a = jnp.where(dmask_ref[2] != 0, p_64, jnp.where(dmask_ref[3] != 0, a_near, p_32))
            a_s[rows, :] = a.astype(BF16)
            kv[j] = lax.dot_general(vb[rows, :], kend_s[rows, :], _TN, preferred_element_type=F32)

        sizes = [EDGE_ROWS] + [PROJ_ROWS] * ((seq - 2 * EDGE_ROWS) // PROJ_ROWS) + [EDGE_ROWS]
        blocks = [(sum(sizes[:i]), size) for i, size in enumerate(sizes)]

        def local_block(start, size):
            activation_block(start, size)
            for m in range(size // (2 * C)):
                prefix_pair(start // (2 * C) + m)
            for c in range(size // C):
                operand_chunk(start // C + c)
                product_chunk(start // C + c)

        proj_block(*blocks[0])
        for prev, cur in zip(blocks[:-1], blocks[1:]):
            proj_block(*cur)
            local_block(*prev)
        local_block(*blocks[-1])

        def fwd_body(j, s):
            stt[j, :, 0:HEAD_DIM] = s.astype(BF16)
            return s * dec[j, 0:1, 0:HEAD_DIM] + kv[j, :, 0:HEAD_DIM]

        lax.fori_loop(0, n_chunks, fwd_body, jnp.zeros((HEAD_DIM, HEAD_DIM), F32))

        def bwd_body(i, s):
            j = n_chunks - 1 - i
            stt[j, :, HEAD_DIM:2 * HEAD_DIM] = s.astype(BF16)
            return s * dec[j, 0:1, HEAD_DIM:2 * HEAD_DIM] + kv[j, :, HEAD_DIM:2 * HEAD_DIM]

        lax.fori_loop(0, n_chunks, bwd_body, jnp.zeros((HEAD_DIM, HEAD_DIM), F32))

        for j in range(n_chunks):
            st[j] = stt[j].T

        grp = h // (HEADS // len(POOL_WINDOWS))
        half = sum(jnp.where(grp == i, w // 2, 0) for i, w in enumerate(POOL_WINDOWS))
        t_loc = lax.broadcasted_iota(jnp.int32, (C, C + 2 * HALO), 0)
        s_ext = lax.broadcasted_iota(jnp.int32, (C, C + 2 * HALO), 1)
        off = s_ext - HALO - t_loc
        band = jnp.where((off >= 1 - half) & (off <= half), 1.0, 0.0).astype(BF16)
        inv_w = sum(jnp.where(grp == i, 1.0 / w, 0.0) for i, w in enumerate(POOL_WINDOWS))

        def out_chunk(j, o_intra, clipped):
            start = j * C
            rows = pl.ds(start, C)
            o = o_intra + jnp.dot(qd[rows, :], st[j], preferred_element_type=F32)
            ms = jnp.mean(o * o, axis=-1, keepdims=True)
            ya_ref[rows, :] = (o * lax.rsqrt(ms + RMS_EPS) * ogs[rows, :]).astype(ya_ref.dtype)
            ws2 = jnp.dot(band, pbuf[pl.ds(start + C - HALO, C + 2 * HALO), :], preferred_element_type=F32)
            ws = ws2[:, 0:128] + ws2[:, 128:256]
            if clipped:
                t = j * C + lax.broadcasted_iota(jnp.int32, (C, HEAD_DIM), 0)
                cnt = jnp.minimum(t + half + 1, seq) - jnp.maximum(t - half + 1, 0)
                mean = ws / cnt.astype(F32)
            else:
                mean = ws * inv_w
            yp_ref[rows, :] = (mean - raw_s[rows, 5 * HEAD_DIM:6 * HEAD_DIM]).astype(yp_ref.dtype)

        zero_v = jnp.zeros((C, HEAD_DIM), BF16)
        for m in range(n_chunks // 2):
            r0, r1 = pl.ds(2 * m * C, C), pl.ds((2 * m + 1) * C, C)
            v_diag = jnp.concatenate([jnp.concatenate([vb[r0, :], zero_v], axis=1),
                                      jnp.concatenate([zero_v, vb[r1, :]], axis=1)], axis=0)
            o_pair = jnp.dot(jnp.concatenate([a_s[r0, :], a_s[r1, :]], axis=1), v_diag,
                             preferred_element_type=F32)
            for i in range(2):
                j = 2 * m + i
                out_chunk(j, o_pair[:, i * HEAD_DIM:(i + 1) * HEAD_DIM], clipped=j in (0, n_chunks - 1))

    @pl.when(h == 0)
    def _():
        run_head(True)

    @pl.when(h != 0)
    def _():
        run_head(False)


def _hgrn_pool(x3, g_mix, w_l, lb_rows, norm_g, tri, dmask, layer):
    B, S, D = x3.shape
    n_lb_rows = lb_rows.shape[0] // 2
    n_chunks = S // CHUNK
    row_f32 = lambda: pltpu.VMEM((S, HEAD_DIM), F32)
    out_spec = pl.BlockSpec((None, S, HEAD_DIM), lambda b, h: (b, 0, h))
    return pl.pallas_call(
        functools.partial(_hgrn_kernel, seq=S, n_lb_rows=n_lb_rows, layer=layer),
        grid=(B, HEADS),
        in_specs=[
            pl.BlockSpec((None, S, D), lambda b, h: (b, 0, 0)),
            pl.BlockSpec((1, D), lambda b, h: (0, 0)),
            *[pl.BlockSpec((D, HEAD_DIM), functools.partial(lambda b, h, i: (0, i * HEADS + h), i=i))
              for i in range(6)],
            pl.BlockSpec((2 * n_lb_rows, HEAD_DIM), lambda b, h: (0, h)),
            pl.BlockSpec((1, HEAD_DIM), lambda b, h: (0, h)),
            pl.BlockSpec((2, CHUNK, 2 * CHUNK), lambda b, h: (0, 0, 0)),
            pl.BlockSpec((4, CHUNK, CHUNK), lambda b, h: (0, 0, 0)),
        ],
        out_specs=[out_spec, out_spec],
        out_shape=[jax.ShapeDtypeStruct((B, S, D), BF16), jax.ShapeDtypeStruct((B, S, D), BF16)],
        scratch_shapes=[
            pltpu.VMEM((S, D), BF16),
            pltpu.VMEM((D, 6 * HEAD_DIM), BF16),
            pltpu.VMEM((S, 6 * HEAD_DIM), F32),
            row_f32(), row_f32(), row_f32(),
            pltpu.VMEM((2, n_chunks // 2, 2 * CHUNK, 2 * HEAD_DIM), BF16),
            pltpu.VMEM((S, HEAD_DIM), BF16),
            row_f32(),
            pltpu.VMEM((S + 2 * CHUNK, 2 * HEAD_DIM), BF16),
            pltpu.VMEM((2, n_chunks, CHUNK, HEAD_DIM), F32),
            pltpu.VMEM((2, S, HEAD_DIM), BF16),
            pltpu.VMEM((2, HEAD_DIM, S), BF16),
            pltpu.VMEM((S, 2 * HEAD_DIM), BF16),
            pltpu.VMEM((HEAD_DIM, S), BF16), pltpu.VMEM((HEAD_DIM, S), BF16),
            pltpu.VMEM((S, 2 * HEAD_DIM), BF16),
            pltpu.VMEM((S, CHUNK), BF16),
            pltpu.VMEM((S, 2 * HEAD_DIM), BF16),
            pltpu.VMEM((n_chunks, HEAD_DIM, 2 * HEAD_DIM), F32),
            pltpu.VMEM((n_chunks, 8, 2 * HEAD_DIM), F32),
            pltpu.VMEM((n_chunks, HEAD_DIM, 2 * HEAD_DIM), BF16),
            pltpu.VMEM((n_chunks, 2 * HEAD_DIM, HEAD_DIM), BF16),
        ],
        compiler_params=pltpu.CompilerParams(
            dimension_semantics=("arbitrary", "arbitrary"),
            vmem_limit_bytes=V7X_VMEM_LIMIT),
        name="hgrn_pool",
    )(x3, g_mix, *([w_l] * 6), lb_rows, norm_g, tri, dmask)


def _merge_kernel(x_ref, gmix_ref, ya_ref, yp_ref, wg_ref, pw_ref, ps_ref, wa_ref, wb_ref, wo_ref, h_ref):
    D = x_ref.shape[1]
    n_grp = pw_ref.shape[0]
    gw = D // n_grp
    u = _rmsnorm_bf16(x_ref[...], gmix_ref[...])
    gates = jnp.dot(u, wg_ref[...], preferred_element_type=F32)
    yb = jnp.concatenate(
        [jnp.dot(yp_ref[:, g * gw:(g + 1) * gw], pw_ref[g], preferred_element_type=F32)
         for g in range(n_grp)], axis=1)
    yb = (yb * ps_ref[...]).astype(BF16)
    za = jnp.dot(ya_ref[...], wa_ref[...], preferred_element_type=F32)
    zb = jnp.dot(yb, wb_ref[...], preferred_element_type=F32)
    merged = _sigmoid(gates[:, 0:D]) * za + _sigmoid(gates[:, D:2 * D]) * zb
    h_ref[...] = x_ref[...] + jnp.dot(merged.astype(BF16), wo_ref[...], preferred_element_type=F32)


def _const_spec(shape):
    return pl.BlockSpec(shape, lambda i: (0,) * len(shape), pipeline_mode=pl.Buffered(1))


def _merge(x2, g_mix, ya2, yp2, w_l, pool_w, pool_scale, w_a, w_b, w_o, tile):
    T, D = x2.shape
    tok = lambda: pl.BlockSpec((tile, D), lambda i: (i, 0))
    return pl.pallas_call(
        _merge_kernel,
        grid=(T // tile,),
        in_specs=[tok(), _const_spec((1, D)), tok(), tok(),
                  pl.BlockSpec((D, 2 * D), lambda i: (0, w_l.shape[1] // (2 * D) - 1),
                               pipeline_mode=pl.Buffered(1)),
                  _const_spec(pool_w.shape), _const_spec((1, D)),
                  _const_spec(w_a.shape), _const_spec(w_b.shape), _const_spec(w_o.shape)],
        out_specs=tok(),
        out_shape=jax.ShapeDtypeStruct((T, D), F32),
        compiler_params=pltpu.CompilerParams(
            dimension_semantics=("arbitrary",), vmem_limit_bytes=V7X_VMEM_LIMIT),
        name="merge",
    )(x2, g_mix, ya2, yp2, w_l, pool_w, pool_scale.reshape(1, D), w_a, w_b, w_o)


def _ffn_kernel(h_ref, g_ref, wi_ref, wo_ref, gfin_ref, o_ref, *, final_norm):
    h = h_ref[...]
    u = _rmsnorm_bf16(h, g_ref[...])
    d_ff = wo_ref.shape[0]
    gate = jnp.dot(u, wi_ref[:, 0:d_ff], preferred_element_type=F32)
    up = jnp.dot(u, wi_ref[:, d_ff:2 * d_ff], preferred_element_type=F32)
    act = (gate * _sigmoid(gate) * up).astype(BF16)
    acc = h + jnp.dot(act, wo_ref[...], preferred_element_type=F32)
    if final_norm:
        ms = jnp.mean(acc * acc, axis=-1, keepdims=True)
        acc = acc * lax.rsqrt(ms + RMS_EPS) * gfin_ref[...]
    o_ref[...] = acc


def _ffn(h2, g_ffn, w_in, w_out, g_final, tile, final_norm):
    T, D = h2.shape
    tok = lambda: pl.BlockSpec((tile, D), lambda i: (i, 0))
    return pl.pallas_call(
        functools.partial(_ffn_kernel, final_norm=final_norm),
        grid=(T // tile,),
        in_specs=[tok(), _const_spec((1, D)), _const_spec(w_in.shape), _const_spec(w_out.shape),
                  _const_spec((1, D))],
        out_specs=tok(),
        out_shape=jax.ShapeDtypeStruct((T, D), F32),
        compiler_params=pltpu.CompilerParams(
            dimension_semantics=("arbitrary",), vmem_limit_bytes=V7X_VMEM_LIMIT),
        name="ffn",
    )(h2, g_ffn.reshape(1, D), w_in, w_out, g_final.reshape(1, D))


def kernel(x, g_mix, w_in, lb_logits, hgrn_norm_g, pool_w, pool_scale, w_branch_a, w_branch_b,
           w_out, g_ffn, w_ffn_in, w_ffn_out, g_final):
    B, S, D = x.shape
    depth = w_in.shape[0]
    T = B * S
    assert D == HEADS * HEAD_DIM
    assert (S - 2 * EDGE_ROWS) % PROJ_ROWS == 0 and S >= 2 * EDGE_ROWS and max(POOL_WINDOWS) // 2 <= HALO
    assert w_ffn_out.shape[1] % MXU_WIDTH == 0
    assert w_in.shape[2] == 8 * D
    tri_np, dmask_np = _chunk_constants()
    tri = jnp.asarray(tri_np, BF16)
    dmask = jnp.asarray(dmask_np, jnp.int32)
    lb_rows = lb_logits.astype(F32).reshape(2 * (depth + 1), D)

    h2 = x.reshape(T, D)
    for l in range(depth):
        w_l = w_in[l].astype(BF16)
        g_mix_l = g_mix[l].reshape(1, D)
        ya, yp = _hgrn_pool(h2.reshape(B, S, D), g_mix_l, w_l, lb_rows, hgrn_norm_g[l].reshape(1, D),
                            tri, dmask, layer=l)
        h2 = _merge(h2, g_mix_l, ya.reshape(T, D), yp.reshape(T, D), w_l,
                    pool_w[l].astype(BF16), pool_scale[l], w_branch_a[l].astype(BF16),
                    w_branch_b[l].astype(BF16), w_out[l].astype(BF16), tile=512)
        h2 = _ffn(h2, g_ffn[l], w_ffn_in[l].astype(BF16), w_ffn_out[l].astype(BF16), g_final,
                  tile=512, final_norm=(l == depth - 1))
    return h2.reshape(B, S, D)
```
